```python
import jax
import jax.numpy as jnp
from jax import lax
import numpy as np


D_MODEL = 4096
BATCH = 4
SEQ = 2048
DEPTH = 2

HEAD_DIM = 128
N_BRANCH = 4
BRANCH_WIDTH = D_MODEL // N_BRANCH
SGU_CHUNK = 128
SGU_GROUP_WIDTH = 128
SGU_GROUPS = BRANCH_WIDTH // SGU_GROUP_WIDTH
POOL_WINDOWS = (2, 4, 8, 16)
POOL_GROUP_WIDTH = BRANCH_WIDTH // len(POOL_WINDOWS)
DIL_PAIRS = ((128, 1), (512, 4), (2048, 16))
DIL_HEADS = BRANCH_WIDTH // HEAD_DIM
DIL_BLOCK = 128
MOBA_HEADS = BRANCH_WIDTH // HEAD_DIM
MOBA_BLOCK = 256
MOBA_TOPK = 3
MOBA_QCHUNK = 32
ALIBI_SETS = len(DIL_PAIRS) + 1
N_ALIBI_HEADS = len(DIL_PAIRS) * DIL_HEADS + MOBA_HEADS
D_FF = 256 * (-(-8 * D_MODEL // (3 * 256)))
CONV_WIDTH = 3
N_MOD = 6
NORM_EPS = 1e-6
NEG_INF = -1e30
COLS_A = 2 * BRANCH_WIDTH
COLS_B = BRANCH_WIDTH
COLS_C = 3 * len(DIL_PAIRS) * DIL_HEADS * HEAD_DIM
COLS_D = 3 * MOBA_HEADS * HEAD_DIM
IN_COLS = COLS_A + COLS_B + COLS_C + COLS_D

kernel_name = 'hybrid_gated_multimixer_decoder'


def rmsnorm(x, g):
    xf = x.astype(jnp.float32)
    y = xf * lax.rsqrt(jnp.mean(xf * xf, axis=-1, keepdims=True) + NORM_EPS)
    return (y * g.astype(jnp.float32)).astype(x.dtype)


def alibi_slopes():
    n = N_ALIBI_HEADS
    return jnp.asarray(2.0 ** (-8.0 * np.arange(1, n + 1) / n), dtype=jnp.float32)


def softmax_stats(s, axes):
    m = jnp.max(s, axis=axes, keepdims=True)
    p = jnp.exp(s - m)
    den = jnp.sum(p, axis=axes, keepdims=True)
    return p, den, m


def spatial_gating(z, norm_g, w_s, b_s):
    bsz, seq, _ = z.shape
    u, v = jnp.split(jax.nn.gelu(z), 2, axis=-1)
    v = rmsnorm(v, norm_g)
    n_chunk = seq // SGU_CHUNK
    v = v.reshape(bsz, n_chunk, SGU_CHUNK, SGU_GROUPS, SGU_GROUP_WIDTH)
    causal = jnp.tril(jnp.ones((SGU_CHUNK, SGU_CHUNK), dtype=bool))
    w_c = jnp.where(causal[None], w_s, 0).astype(v.dtype)
    mixed = jnp.einsum('gts,bnsgc->bntgc', w_c, v) + b_s.T[:, :, None].astype(v.dtype)
    return u * mixed.reshape(bsz, seq, BRANCH_WIDTH)


def multiscale_pool(z, w_pool, scale):
    bsz, seq, _ = z.shape
    n_g = len(POOL_WINDOWS)
    zg = z.reshape(bsz, seq, n_g, POOL_GROUP_WIDTH).astype(jnp.float32)
    csum = jnp.cumsum(zg, axis=1)
    prev = jnp.stack([jnp.pad(csum[:, :, g], ((0, 0), (w, 0), (0, 0)))[:, :seq]
                      for g, w in enumerate(POOL_WINDOWS)], axis=2)
    t = jnp.arange(seq)[:, None]
    count = jnp.minimum(t + 1, jnp.asarray(POOL_WINDOWS)[None, :]).astype(jnp.float32)
    mixed = ((csum - prev) / count[None, :, :, None] - zg).astype(z.dtype)
    y = jnp.einsum('bsgc,gce->bsge', mixed, w_pool)
    return y.reshape(bsz, seq, BRANCH_WIDTH) * scale


def dilated_window_attention(q, k, v, window, dilation, slopes):
    bsz, seq, nh, hd = q.shape
    span = window // dilation
    sub = seq // dilation
    n_blk = -(-sub // DIL_BLOCK)
    pad = n_blk * DIL_BLOCK - sub

    def to_blocks(a):
        a = a.reshape(bsz, sub, dilation, nh, hd).transpose(0, 2, 1, 3, 4)
        a = jnp.pad(a, ((0, 0), (0, 0), (0, pad), (0, 0), (0, 0)))
        return a.reshape(bsz, dilation, n_blk, DIL_BLOCK, nh, hd)

    def band(a):
        prev = jnp.pad(a, ((0, 0), (0, 0), (1, 0), (0, 0), (0, 0), (0, 0)))[:, :, :-1]
        return jnp.concatenate([prev, a], axis=3)

    qb = to_blocks(q)
    kband = band(to_blocks(k))
    vband = band(to_blocks(v)).astype(jnp.float32)
    s = jnp.einsum('brnqhd,brnkhd->brhnqk', qb, kband).astype(jnp.float32) * (hd ** -0.5)
    qi = jnp.arange(DIL_BLOCK)[:, None]
    ki = jnp.arange(2 * DIL_BLOCK)[None, :]
    steps = qi - ki + DIL_BLOCK
    key_sub = jnp.arange(n_blk)[:, None, None] * DIL_BLOCK + ki[None] - DIL_BLOCK
    valid = ((steps >= 0) & (steps <= span))[None] & (key_sub >= 0)
    bias = -slopes[:, None, None] * (dilation * steps).astype(jnp.float32)
    s = jnp.where(valid[None, None, None], s + bias[None, None, :, None], NEG_INF)
    p, den, m = softmax_stats(s, -1)
    o = jnp.einsum('brhnqk,brnkhd->brhnqd', p, vband) / den
    lse = (m + jnp.log(den))[..., 0]
    o = o.reshape(bsz, dilation, nh, n_blk * DIL_BLOCK, hd)[:, :, :, :sub]
    o = o.transpose(0, 3, 1, 2, 4).reshape(bsz, seq, nh, hd)
    lse = lse.reshape(bsz, dilation, nh, n_blk * DIL_BLOCK)[..., :sub]
    lse = lse.transpose(0, 3, 1, 2).reshape(bsz, seq, nh)
    return o, lse


def dilated_mixer(zc, slopes):
    bsz, seq, _ = zc.shape
    qkv = zc.reshape(bsz, seq, 3, len(DIL_PAIRS), DIL_HEADS, HEAD_DIM)
    outs, lses = [], []
    for g, (window, dilation) in enumerate(DIL_PAIRS):
        o, l = dilated_window_attention(qkv[:, :, 0, g], qkv[:, :, 1, g], qkv[:, :, 2, g],
                                        window, dilation, slopes[g::ALIBI_SETS])
        outs.append(o)
        lses.append(l)
    wts = jax.nn.softmax(jnp.stack(lses), axis=0)
    o = jnp.sum(wts[..., None] * jnp.stack(outs), axis=0)
    return o.reshape(bsz, seq, BRANCH_WIDTH).astype(zc.dtype)


def moba_mixer(zd, slopes):
    bsz, seq, _ = zd.shape
    nh, hd, blk, qc = MOBA_HEADS, HEAD_DIM, MOBA_BLOCK, MOBA_QCHUNK
    n_blk = -(-seq // blk)
    seq_p = n_blk * blk
    qkv = jnp.pad(zd, ((0, 0), (0, seq_p - seq), (0, 0))).reshape(bsz, seq_p, 3, nh, hd)
    q, k, v = qkv[:, :, 0], qkv[:, :, 1], qkv[:, :, 2]
    scale = hd ** -0.5
    qb = q.reshape(bsz, n_blk, blk, nh, hd)
    kb = k.reshape(bsz, n_blk, blk, nh, hd)
    vb = v.reshape(bsz, n_blk, blk, nh, hd)

    ii = jnp.arange(blk)
    dist = ii[:, None] - ii[None, :]
    s_own = jnp.einsum('bnqhd,bnkhd->bhnqk', qb, kb).astype(jnp.float32) * scale
    s_own = jnp.where(dist >= 0, s_own - slopes[:, None, None, None] * dist.astype(jnp.float32), NEG_INF)
    p, den, m = softmax_stats(s_own, -1)
    o_own = jnp.einsum('bhnqk,bnkhd->bhnqd', p, vb.astype(jnp.float32)) / den
    o_own = o_own.reshape(bsz, nh, seq_p, hd)
    lse_own = (m + jnp.log(den))[..., 0].reshape(bsz, nh, seq_p)

    k_mean = jnp.mean(kb.astype(jnp.float32), axis=2)
    gate = jnp.einsum('bshd,bnhd->bhsn', q.astype(jnp.float32), k_mean)
    t = jnp.arange(seq_p)
    q_blk = t // blk
    fully_past = jnp.arange(n_blk)[None, :] < q_blk[:, None]
    gate = jnp.where(fully_past, gate, NEG_INF)
    top = min(MOBA_TOPK, n_blk)
    _, sel = lax.top_k(gate, top)
    sel_valid = jnp.arange(top)[None, :] < q_blk[:, None]

    k_t = kb.transpose(0, 3, 1, 2, 4)
    v_t = vb.transpose(0, 3, 1, 2, 4)
    n_qc = seq_p // qc
    q_c = q.transpose(0, 2, 1, 3).reshape(bsz, nh, n_qc, qc, hd).transpose(2, 0, 1, 3, 4)
    sel_c = sel.reshape(bsz, nh, n_qc, qc, top).transpose(2, 0, 1, 3, 4)
    valid_c = sel_valid.reshape(n_qc, qc, top)
    t_c = t.reshape(n_qc, qc)
    gather = jax.vmap(jax.vmap(lambda a, i: a[i]))

    def selected_blocks(args):
        qq, ss, vv, tt = args
        flat = ss.reshape(bsz, nh, qc * top)
        kg = gather(k_t, flat).reshape(bsz, nh, qc, top, blk, hd)
        vg = gather(v_t, flat).reshape(bsz, nh, qc, top, blk, hd).astype(jnp.float32)
        s = jnp.einsum('bhqd,bhqrkd->bhqrk', qq, kg).astype(jnp.float32) * scale
        key_pos = ss[..., None] * blk + jnp.arange(blk)
        d = (tt[None, None, :, None, None] - key_pos).astype(jnp.float32)
        s = jnp.where(vv[None, None, :, :, None], s - slopes[None, :, None, None, None] * d, NEG_INF)
        p, den, m = softmax_stats(s, (-2, -1))
        o = jnp.einsum('bhqrk,bhqrkd->bhqd', p, vg) / den[..., 0]
        return o, (m + jnp.log(den))[..., 0, 0]

    o_sel, lse_sel = lax.map(selected_blocks, (q_c, sel_c, valid_c, t_c))
    o_sel = o_sel.transpose(1, 2, 0, 3, 4).reshape(bsz, nh, seq_p, hd)
    lse_sel = lse_sel.transpose(1, 2, 0, 3).reshape(bsz, nh, seq_p)

    m = jnp.maximum(lse_own, lse_sel)
    w_own = jnp.exp(lse_own - m)
    w_sel = jnp.exp(lse_sel - m)
    o = (w_own[..., None] * o_own + w_sel[..., None] * o_sel) / (w_own + w_sel)[..., None]
    o = o.transpose(0, 2, 1, 3)[:, :seq].reshape(bsz, seq, BRANCH_WIDTH)
    return o.astype(zd.dtype)


def hybrid_layer(x, c, norm1_g, w_ada, b_ada, w_in, sgu_norm_g, sgu_w, sgu_b, pool_w, pool_scale,
                 merge_w, merge_b, branch_w, out_w, norm2_g, ffn_wg, ffn_wu, conv_w, conv_b, ffn_wd,
                 slopes):
    bsz = x.shape[0]
    mod = (c @ w_ada + b_ada).reshape(bsz, N_MOD, 1, D_MODEL)
    shift1, scale1, gate1 = mod[:, 0], mod[:, 1], mod[:, 2]
    shift2, scale2, gate2 = mod[:, 3], mod[:, 4], mod[:, 5]

    h = rmsnorm(x, norm1_g) * (1 + scale1) + shift1
    z = h @ w_in
    z_a, z_b, z_c, z_d = jnp.split(z, [COLS_A, COLS_A + COLS_B, COLS_A + COLS_B + COLS_C], axis=-1)
    y_a = spatial_gating(z_a, sgu_norm_g, sgu_w, sgu_b)
    y_b = multiscale_pool(z_b, pool_w, pool_scale)
    y_c = dilated_mixer(z_c, slopes)
    y_d = moba_mixer(z_d, slopes[len(DIL_PAIRS)::ALIBI_SETS])
    branches = jnp.stack([y_a, y_b, y_c, y_d], axis=2)
    gates = jax.nn.sigmoid(jnp.einsum('bsd,dne->bsne', h, merge_w) + merge_b)
    projected = jnp.einsum('bsnw,nwe->bsne', branches, branch_w)
    merged = jnp.sum(gates * projected, axis=2)
    x = x + gate1 * (merged @ out_w)

    h2 = rmsnorm(x, norm2_g) * (1 + scale2) + shift2
    a = lax.conv_general_dilated(h2 @ ffn_wg, conv_w[:, None, :], window_strides=(1,),
                                 padding=[(CONV_WIDTH - 1, 0)],
                                 dimension_numbers=('NWC', 'WIO', 'NWC'),
                                 feature_group_count=D_FF) + conv_b
    f = jax.nn.gelu(a) * (h2 @ ffn_wu)
    return x + gate2 * (f @ ffn_wd)


def setup_inputs(seed: int = 0) -> dict:
    key = jax.random.key(seed)
    ks = jax.random.split(key, 22)
    L, D, W, P = DEPTH, D_MODEL, BRANCH_WIDTH, POOL_GROUP_WIDTH

    def nrm(i, shape, std):
        return jax.random.normal(ks[i], shape, jnp.float32) * std

    return {
        'x': nrm(0, (BATCH, SEQ, D), 1.0),
        'c': nrm(1, (BATCH, D), 1.0),
        'norm1_g': 1.0 + nrm(2, (L, D), 0.05),
        'w_ada': nrm(3, (L, D, N_MOD * D), 0.5 * D ** -0.5),
        'b_ada': nrm(4, (L, N_MOD * D), 0.02),
        'w_in': nrm(5, (L, D, IN_COLS), D ** -0.5),
        'sgu_norm_g': 1.0 + nrm(6, (L, W), 0.05),
        'sgu_w': nrm(7, (L, SGU_GROUPS, SGU_CHUNK, SGU_CHUNK), 0.5 * SGU_CHUNK ** -0.5),
        'sgu_b': 1.0 + nrm(8, (L, SGU_GROUPS, SGU_CHUNK), 0.1),
        'pool_w': nrm(9, (L, len(POOL_WINDOWS), P, P), P ** -0.5),
        'pool_scale': 1.0 + nrm(10, (L, W), 0.1),
        'merge_w': nrm(11, (L, D, N_BRANCH, D), D ** -0.5),
        'merge_b': nrm(12, (L, N_BRANCH, D), 0.02),
        'branch_w': nrm(13, (L, N_BRANCH, W, D), W ** -0.5),
        'out_w': nrm(14, (L, D, D), D ** -0.5),
        'norm2_g': 1.0 + nrm(15, (L, D), 0.05),
        'ffn_wg': nrm(16, (L, D, D_FF), D ** -0.5),
        'ffn_wu': nrm(17, (L, D, D_FF), D ** -0.5),
        'conv_w': nrm(18, (L, CONV_WIDTH, D_FF), CONV_WIDTH ** -0.5),
        'conv_b': nrm(19, (L, D_FF), 0.02),
        'ffn_wd': nrm(20, (L, D_FF, D), D_FF ** -0.5),
        'final_g': 1.0 + nrm(21, (D,), 0.05),
    }


def reference(x, c, norm1_g, w_ada, b_ada, w_in, sgu_norm_g, sgu_w, sgu_b, pool_w, pool_scale,
              merge_w, merge_b, branch_w, out_w, norm2_g, ffn_wg, ffn_wu, conv_w, conv_b, ffn_wd,
              final_g):
    slopes = alibi_slopes()
    for i in range(DEPTH):
        x = hybrid_layer(x, c, norm1_g[i], w_ada[i], b_ada[i], w_in[i], sgu_norm_g[i], sgu_w[i],
                         sgu_b[i], pool_w[i], pool_scale[i], merge_w[i], merge_b[i], branch_w[i],
                         out_w[i], norm2_g[i], ffn_wg[i], ffn_wu[i], conv_w[i], conv_b[i], ffn_wd[i],
                         slopes)
    return rmsnorm(x, final_g)
```

```python
import functools

import jax
import jax.numpy as jnp
import numpy as np
from jax import lax
from jax.experimental import pallas as pl
from jax.experimental.pallas import tpu as pltpu

BF16 = jnp.bfloat16
F32 = jnp.float32

HEAD_DIM = 128
N_HEADS = 8
BRANCH_WIDTH = N_HEADS * HEAD_DIM
SGU_CHUNK = 128
SGU_GROUPS = 8
POOL_WINDOWS = (2, 4, 8, 16)
POOL_GROUP_WIDTH = BRANCH_WIDTH // len(POOL_WINDOWS)
DIL_DILATIONS = (1, 4, 16)
DIL_BLOCK = 128
MOBA_BLOCK = 256
MOBA_TOPK = 3
N_ALIBI_HEADS = 32
ALIBI_SETS = 4
N_MOD = 6
NORM_EPS = 1e-6
NEG_INF = -1e30
ATTN_SCALE = HEAD_DIM ** -0.5

V7X_VMEM_BYTES = 64 * 2 ** 20
VMEM_LIMIT_CAP = 56 * 2 ** 20
MIB = 2 ** 20
VMEM_SLACK = 8 * MIB


def _alibi_slopes():
    n = N_ALIBI_HEADS
    return np.asarray(2.0 ** (-8.0 * np.arange(1, n + 1) / n), dtype=np.float32)


def _params(n_axes, vmem_bytes):
    limit = min(VMEM_LIMIT_CAP, max(32 * MIB, int(vmem_bytes)))
    return pltpu.CompilerParams(dimension_semantics=("arbitrary",) * n_axes,
                                vmem_limit_bytes=limit)


def _single_buffered(block_shape, index_map):
    return pl.BlockSpec(block_shape, index_map, pipeline_mode=pl.Buffered(1))


def _ada_kernel(c_ref, w_ref, b_ref, o_ref):
    w = w_ref[...].astype(BF16)
    o_ref[...] = jnp.dot(c_ref[...], w, preferred_element_type=F32) + b_ref[...]


def _ada_mod(c_pad, w_ada, b_ada, layer):
    d = c_pad.shape[1]
    n = w_ada.shape[2]
    tn = 512
    return pl.pallas_call(
        _ada_kernel,
        grid=(n // tn,),
        in_specs=[pl.BlockSpec((8, d), lambda j: (0, 0)),
                  pl.BlockSpec((None, d, tn), lambda j: (layer, 0, j)),
                  pl.BlockSpec((None, 1, tn), lambda j: (layer, 0, j))],
        out_specs=pl.BlockSpec((8, tn), lambda j: (0, j)),
        out_shape=jax.ShapeDtypeStruct((8, n), F32),
        compiler_params=_params(1, 2 * d * tn * 4 + 3 * d * tn * 2 + 8 * MIB),
        name="ada_mod",
    )(c_pad, w_ada, b_ada)


def _norm_mod_kernel(x_ref, g_ref, sc_ref, sh_ref, o_ref):
    x = x_ref[...]
    y = x * lax.rsqrt(jnp.mean(x * x, axis=-1, keepdims=True) + NORM_EPS) * g_ref[...]
    o_ref[...] = (y * (1.0 + sc_ref[...]) + sh_ref[...]).astype(o_ref.dtype)


def _norm_kernel(x_ref, g_ref, o_ref):
    x = x_ref[...]
    y = x * lax.rsqrt(jnp.mean(x * x, axis=-1, keepdims=True) + NORM_EPS) * g_ref[...]
    o_ref[...] = y.astype(o_ref.dtype)


def _norm_mod(x, g, layer, mod, shift_idx, scale_idx, seq):
    t, d = x.shape
    tm = 256
    per_seq = seq // tm
    return pl.pallas_call(
        _norm_mod_kernel,
        grid=(t // tm,),
        in_specs=[pl.BlockSpec((tm, d), lambda i: (i, 0)),
                  pl.BlockSpec((None, 1, d), lambda i: (layer, 0, 0)),
                  pl.BlockSpec((None, 1, d), lambda i: ((i // per_seq) * N_MOD + scale_idx, 0, 0)),
                  pl.BlockSpec((None, 1, d), lambda i: ((i // per_seq) * N_MOD + shift_idx, 0, 0))],
        out_specs=pl.BlockSpec((tm, d), lambda i: (i, 0)),
        out_shape=jax.ShapeDtypeStruct((t, d), BF16),
        compiler_params=_params(1, 6 * tm * d * 4 + 8 * MIB),
        name="norm_mod",
    )(x, g, mod, mod)


def _final_norm(x, g):
    t, d = x.shape
    tm = 256
    return pl.pallas_call(
        _norm_kernel,
        grid=(t // tm,),
        in_specs=[pl.BlockSpec((tm, d), lambda i: (i, 0)),
                  pl.BlockSpec((1, d), lambda i: (0, 0))],
        out_specs=pl.BlockSpec((tm, d), lambda i: (i, 0)),
        out_shape=jax.ShapeDtypeStruct((t, d), F32),
        compiler_params=_params(1, 6 * tm * d * 4 + 8 * MIB),
        name="final_norm",
    )(x, g)


def _mm_kernel(a_ref, w_ref, o_ref):
    o_ref[...] = jnp.dot(a_ref[...], w_ref[...], preferred_element_type=F32).astype(o_ref.dtype)


def _mm_residual_kernel(a_ref, w_ref, x_ref, g_ref, o_ref):
    acc = jnp.dot(a_ref[...], w_ref[...], preferred_element_type=F32)
    o_ref[...] = x_ref[...] + g_ref[...] * acc


def _matmul(a, w, layer, col0, n_cols, out_dtype, tm, tn):
    m, k = a.shape
    col_block0 = col0 // tn
    assert col0 % tn == 0 and n_cols % tn == 0
    vmem = 2 * tm * k * 2 + 2 * k * tn * 2 + 3 * tm * tn * 4 + VMEM_SLACK
    return pl.pallas_call(
        _mm_kernel,
        grid=(m // tm, n_cols // tn),
        in_specs=[pl.BlockSpec((tm, k), lambda i, j: (i, 0)),
                  pl.BlockSpec((None, k, tn), lambda i, j: (layer, 0, col_block0 + j))],
        out_specs=pl.BlockSpec((tm, tn), lambda i, j: (i, j)),
        out_shape=jax.ShapeDtypeStruct((m, n_cols), out_dtype),
        compiler_params=_params(2, vmem),
        name="proj",
    )(a, w)


def _matmul_residual(a, w, layer, x, mod, gate_idx, seq, tm, tn):
    m, k = a.shape
    n = w.shape[2]
    per_seq = seq // tm
    vmem = tm * k * 2 + 2 * k * tn * 2 + 5 * tm * tn * 4 + VMEM_SLACK
    return pl.pallas_call(
        _mm_residual_kernel,
        grid=(m // tm, n // tn),
        in_specs=[_single_buffered((tm, k), lambda i, j: (i, 0)),
                  pl.BlockSpec((None, k, tn), lambda i, j: (layer, 0, j)),
                  pl.BlockSpec((tm, tn), lambda i, j: (i, j)),
                  pl.BlockSpec((None, 1, tn), lambda i, j: ((i // per_seq) * N_MOD + gate_idx, 0, j))],
        out_specs=pl.BlockSpec((tm, tn), lambda i, j: (i, j)),
        out_shape=jax.ShapeDtypeStruct((m, n), F32),
        compiler_params=_params(2, vmem),
        name="proj_residual",
    )(a, w, x, mod)


def _sgu_kernel(z_ref, g_ref, w_ref, bt_ref, o_ref, *, n_chunk):
    width = o_ref.shape[1]
    gw = width // SGU_GROUPS
    ti = lax.broadcasted_iota(jnp.int32, (SGU_CHUNK, SGU_CHUNK), 0)
    si = lax.broadcasted_iota(jnp.int32, (SGU_CHUNK, SGU_CHUNK), 1)
    causal = ti >= si
    w_c = [jnp.where(causal, w_ref[g], 0.0).astype(BF16) for g in range(SGU_GROUPS)]
    for c in range(n_chunk):
        rows = slice(c * SGU_CHUNK, (c + 1) * SGU_CHUNK)
        ge = jax.nn.gelu(z_ref[rows, :])
        u = ge[:, :width]
        v = ge[:, width:]
        v = v * lax.rsqrt(jnp.mean(v * v, axis=-1, keepdims=True) + NORM_EPS) * g_ref[...]
        vb = v.astype(BF16)
        for g in range(SGU_GROUPS):
            cols = slice(g * gw, (g + 1) * gw)
            mixed = jnp.dot(w_c[g], vb[:, cols], preferred_element_type=F32) + bt_ref[:, g:g + 1]
            o_ref[rows, cols] = (u[:, cols] * mixed).astype(o_ref.dtype)


def _sgu(z_ab, norm_g, w_s, b_t, layer):
    t = z_ab.shape[0]
    width = BRANCH_WIDTH
    n_chunk = 4
    tm = n_chunk * SGU_CHUNK
    return pl.pallas_call(
        functools.partial(_sgu_kernel, n_chunk=n_chunk),
        grid=(t // tm,),
        in_specs=[pl.BlockSpec((tm, 2 * width), lambda i: (i, 0)),
                  pl.BlockSpec((None, 1, width), lambda i: (layer, 0, 0)),
                  pl.BlockSpec((None, SGU_GROUPS, SGU_CHUNK, SGU_CHUNK), lambda i: (layer, 0, 0, 0)),
                  pl.BlockSpec((None, SGU_CHUNK, SGU_GROUPS), lambda i: (layer, 0, 0))],
        out_specs=pl.BlockSpec((tm, width), lambda i: (i, 0)),
        out_shape=jax.ShapeDtypeStruct((t, width), BF16),
        compiler_params=_params(1, 32 * MIB),
        name="sgu",
    )(z_ab, norm_g, w_s, b_t)


def _pool_kernel(z_ref, w_ref, sc_ref, o_ref):
    seq = z_ref.shape[0]
    pw = POOL_GROUP_WIDTH
    row = lax.broadcasted_iota(jnp.int32, (seq, pw), 0)
    pos1 = (lax.broadcasted_iota(jnp.int32, (seq, 1), 0) + 1).astype(F32)
    for g, window in enumerate(POOL_WINDOWS):
        cols = slice(g * pw, (g + 1) * pw)
        z = z_ref[:, cols]
        s = z
        shift = 1
        while shift < window:
            s = s + jnp.where(row >= shift, pltpu.roll(s, shift, 0), 0.0)
            shift *= 2
        count = jnp.minimum(pos1, float(window))
        mixed = (s / count - z).astype(BF16)
        y = jnp.dot(mixed, w_ref[g], preferred_element_type=F32) * sc_ref[:, cols]
        o_ref[:, cols] = y.astype(o_ref.dtype)


def _pool(z_ab, w_pool, scale, layer, bsz, seq):
    t = z_ab.shape[0]
    width = BRANCH_WIDTH
    pw = POOL_GROUP_WIDTH
    return pl.pallas_call(
        _pool_kernel,
        grid=(bsz,),
        in_specs=[pl.BlockSpec((seq, width), lambda b: (b, 2)),
                  pl.BlockSpec((None, len(POOL_WINDOWS), pw, pw), lambda b: (layer, 0, 0, 0)),
                  pl.BlockSpec((None, 1, width), lambda b: (layer, 0, 0))],
        out_specs=pl.BlockSpec((seq, width), lambda b: (b, 0)),
        out_shape=jax.ShapeDtypeStruct((t, width), BF16),
        compiler_params=_params(1, 48 * MIB),
        name="pool",
    )(z_ab, w_pool, scale)


def _dilated_kernel(q_ref, k_ref, v_ref, kp_ref, vp_ref, o_ref, lse_ref, kbuf, vbuf, *,
                    slopes, n_seq, q_rows):
    blk = DIL_BLOCK
    n_blk = q_rows // blk
    chunk = pl.program_id(1)
    qi = lax.broadcasted_iota(jnp.int32, (blk, 2 * blk), 0)
    kk = lax.broadcasted_iota(jnp.int32, (blk, 2 * blk), 1)
    steps = qi - kk + blk
    in_window = (steps >= 0) & (steps <= blk)
    steps_f = steps.astype(F32)
    lane = lax.broadcasted_iota(jnp.int32, (blk, HEAD_DIM), 1)
    for s in range(n_seq):
        kbuf[0:blk, :] = kp_ref[s]
        kbuf[blk:, :] = k_ref[s]
        vbuf[0:blk, :] = vp_ref[s]
        vbuf[blk:, :] = v_ref[s]

        def block_body(j, carry, s=s):
            r0 = pl.multiple_of(j * blk, blk)
            key_sub = (chunk * n_blk + j - 1) * blk + kk
            valid = in_window & (key_sub >= 0)
            lse_tile = jnp.zeros((blk, HEAD_DIM), F32)
            for h in range(N_HEADS):
                cols = slice(h * HEAD_DIM, (h + 1) * HEAD_DIM)
                q = q_ref[s, pl.ds(r0, blk), cols]
                k = kbuf[pl.ds(r0, 2 * blk), cols]
                v = vbuf[pl.ds(r0, 2 * blk), cols]
                sc = lax.dot_general(q, k, (((1,), (1,)), ((), ())),
                                     preferred_element_type=F32) * ATTN_SCALE
                sc = jnp.where(valid, sc - slopes[h] * steps_f, NEG_INF)
                m = jnp.max(sc, axis=-1, keepdims=True)
                p = jnp.exp(sc - m)
                den = jnp.sum(p, axis=-1, keepdims=True)
                o = jnp.dot(p.astype(BF16), v, preferred_element_type=F32) / den
                o_ref[s, pl.ds(r0, blk), cols] = o
                lse_tile = jnp.where(lane == h, m + jnp.log(den), lse_tile)
            lse_ref[s, pl.ds(r0, blk), :] = lse_tile
            return carry

        lax.fori_loop(0, n_blk, block_body, 0)


def _dilated_attention(arr, col_blocks, dilation, group):
    n_sub_seq, sub, _ = arr.shape
    width = BRANCH_WIDTH
    q_rows = min(sub, 512)
    n_seq = min(max(1, 1024 // sub), n_sub_seq)
    blocks_per_chunk = q_rows // DIL_BLOCK
    slopes = tuple(float(s) * dilation for s in _alibi_slopes()[group::ALIBI_SETS])
    qc, kc, vc = col_blocks

    def prev_block(c):
        return jnp.maximum(c * blocks_per_chunk - 1, 0)

    return pl.pallas_call(
        functools.partial(_dilated_kernel, slopes=slopes, n_seq=n_seq, q_rows=q_rows),
        grid=(n_sub_seq // n_seq, sub // q_rows),
        in_specs=[pl.BlockSpec((n_seq, q_rows, width), lambda s, c: (s, c, qc)),
                  pl.BlockSpec((n_seq, q_rows, width), lambda s, c: (s, c, kc)),
                  pl.BlockSpec((n_seq, q_rows, width), lambda s, c: (s, c, vc)),
                  pl.BlockSpec((n_seq, DIL_BLOCK, width), lambda s, c: (s, prev_block(c), kc)),
                  pl.BlockSpec((n_seq, DIL_BLOCK, width), lambda s, c: (s, prev_block(c), vc))],
        out_specs=[pl.BlockSpec((n_seq, q_rows, width), lambda s, c: (s, c, 0)),
                   pl.BlockSpec((n_seq, q_rows, HEAD_DIM), lambda s, c: (s, c, 0))],
        out_shape=[jax.ShapeDtypeStruct((n_sub_seq, sub, width), F32),
                   jax.ShapeDtypeStruct((n_sub_seq, sub, HEAD_DIM), F32)],
        scratch_shapes=[pltpu.VMEM((DIL_BLOCK + q_rows, width), BF16),
                        pltpu.VMEM((DIL_BLOCK + q_rows, width), BF16)],
        compiler_params=_params(2, 40 * MIB),
        name="dilated_attn",
    )(arr, arr, arr, arr, arr)


def _dilated_combine_kernel(o0, o1, o2, l0, l1, l2, y_ref):
    for h in range(N_HEADS):
        cols = slice(h * HEAD_DIM, (h + 1) * HEAD_DIM)
        a0 = l0[:, h:h + 1]
        a1 = l1[:, h:h + 1]
        a2 = l2[:, h:h + 1]
        m = jnp.maximum(jnp.maximum(a0, a1), a2)
        e0 = jnp.exp(a0 - m)
        e1 = jnp.exp(a1 - m)
        e2 = jnp.exp(a2 - m)
        y = (e0 * o0[:, cols] + e1 * o1[:, cols] + e2 * o2[:, cols]) / (e0 + e1 + e2)
        y_ref[:, cols] = y.astype(y_ref.dtype)


def _dilated_combine(outs, lses):
    t, width = outs[0].shape
    tm = 512
    o_spec = pl.BlockSpec((tm, width), lambda i: (i, 0))
    l_spec = pl.BlockSpec((tm, HEAD_DIM), lambda i: (i, 0))
    return pl.pallas_call(
        _dilated_combine_kernel,
        grid=(t // tm,),
        in_specs=[o_spec, o_spec, o_spec, l_spec, l_spec, l_spec],
        out_specs=pl.BlockSpec((tm, width), lambda i: (i, 0)),
        out_shape=jax.ShapeDtypeStruct((t, width), BF16),
        compiler_params=_params(1, 32 * MIB),
        name="dilated_combine",
    )(*outs, *lses)


def _dilated_mixer(z_cd, bsz, seq):
    width = BRANCH_WIDTH
    n_groups = len(DIL_DILATIONS)
    z3 = z_cd.reshape(bsz, seq, z_cd.shape[1])
    outs, lses = [], []
    for g, d in enumerate(DIL_DILATIONS):
        sub = seq // d
        if d == 1:
            o, lse = _dilated_attention(z3, (g, n_groups + g, 2 * n_groups + g), d, g)
        else:
            qkv = jnp.stack([z3[:, :, (i * n_groups + g) * width:(i * n_groups + g + 1) * width]
                             for i in range(3)], axis=2)
            qkv = qkv.reshape(bsz, sub, d, 3 * width).transpose(0, 2, 1, 3)
            o, lse = _dilated_attention(qkv.reshape(bsz * d, sub, 3 * width), (0, 1, 2), d, g)
            o = o.reshape(bsz, d, sub, width).transpose(0, 2, 1, 3)
            lse = lse.reshape(bsz, d, sub, HEAD_DIM).transpose(0, 2, 1, 3)
        outs.append(o.reshape(bsz * seq, width))
        lses.append(lse.reshape(bsz * seq, HEAD_DIM))
    return _dilated_combine(outs, lses)


def _moba_kernel(slopes_ref, q_ref, k_ref, v_ref, o_ref):
    blk = MOBA_BLOCK
    seq = k_ref.shape[0]
    n_blk = seq // blk
    head = pl.program_id(1)
    qb = pl.program_id(2)
    slope = slopes_ref[head]
    q = q_ref[...]

    k_mean = jnp.sum(k_ref[...].astype(F32).reshape(n_blk, blk, HEAD_DIM), axis=1) * (1.0 / blk)
    k_mean = jnp.concatenate([k_mean, jnp.zeros((HEAD_DIM - n_blk, HEAD_DIM), F32)], axis=0)
    gate = lax.dot_general(q.astype(F32), k_mean, (((1,), (1,)), ((), ())),
                           precision=lax.Precision.HIGHEST, preferred_element_type=F32)
    lane = lax.broadcasted_iota(jnp.int32, (blk, HEAD_DIM), 1)
    past = lane < qb
    gate = jnp.where(past, gate, NEG_INF)
    rank = jnp.zeros((blk, HEAD_DIM), jnp.int32)
    for m in range(n_blk):
        col = gate[:, m:m + 1]
        ahead = (col > gate) | ((col == gate) & (lane > m))
        rank = rank + ahead.astype(jnp.int32)
    selected = jnp.where((past & (rank < MOBA_TOPK)) | (lane == qb), 1.0, 0.0)

    qi = lax.broadcasted_iota(jnp.int32, (blk, blk), 0)
    ki = lax.broadcasted_iota(jnp.int32, (blk, blk), 1)
    rel = qi - ki

    def block_body(n, carry):
        m_i, l_i, acc = carry
        r0 = pl.multiple_of(n * blk, blk)
        kn = k_ref[pl.ds(r0, blk), :]
        vn = v_ref[pl.ds(r0, blk), :]
        s = lax.dot_general(q, kn, (((1,), (1,)), ((), ())), preferred_element_type=F32) * ATTN_SCALE
        dist = rel + (qb - n) * blk
        picked = jnp.max(jnp.where(lane == n, selected, 0.0), axis=-1, keepdims=True) > 0.0
        s = jnp.where(picked & (dist >= 0), s - slope * dist.astype(F32), NEG_INF)
        m_new = jnp.maximum(m_i, jnp.max(s, axis=-1, keepdims=True))
        alpha = jnp.exp(m_i - m_new)
        p = jnp.exp(s - m_new)
        l_new = alpha * l_i + jnp.sum(p, axis=-1, keepdims=True)
        acc = alpha * acc + jnp.dot(p.astype(BF16), vn, preferred_element_type=F32)
        return m_new, l_new, acc

    init = (jnp.full((blk, 1), NEG_INF, F32), jnp.zeros((blk, 1), F32),
            jnp.zeros((blk, HEAD_DIM), F32))
    _, l_f, acc_f = lax.fori_loop(0, qb + 1, block_body, init)
    o_ref[...] = (acc_f / l_f).astype(o_ref.dtype)


def _moba_mixer(z_cd, bsz, seq):
    width = BRANCH_WIDTH
    z3 = z_cd.reshape(bsz, seq, z_cd.shape[1])
    c0 = 9 * N_HEADS
    slopes = jnp.asarray(_alibi_slopes()[len(DIL_DILATIONS)::ALIBI_SETS])
    kv_spec = lambda off: pl.BlockSpec((None, seq, HEAD_DIM), lambda b, h, i: (b, 0, c0 + off + h))
    out = pl.pallas_call(
        _moba_kernel,
        grid=(bsz, N_HEADS, seq // MOBA_BLOCK),
        in_specs=[pl.BlockSpec(memory_space=pltpu.SMEM),
                  pl.BlockSpec((None, MOBA_BLOCK, HEAD_DIM), lambda b, h, i: (b, i, c0 + h)),
                  kv_spec(N_HEADS), kv_spec(2 * N_HEADS)],
        out_specs=pl.BlockSpec((None, MOBA_BLOCK, HEAD_DIM), lambda b, h, i: (b, i, h)),
        out_shape=jax.ShapeDtypeStruct((bsz, seq, width), BF16),
        compiler_params=_params(3, 32 * MIB),
        name="moba",
    )(slopes, z3, z3, z3)
    return out.reshape(bsz * seq, width)


def _merge_kernel(h_ref, ya_ref, yb_ref, yc_ref, yd_ref, mw0, mw1, mw2, mw3, mb_ref, bw_ref, o_ref):
    h = h_ref[...]
    acc = None
    for b, (y_ref, mw_ref) in enumerate(zip((ya_ref, yb_ref, yc_ref, yd_ref), (mw0, mw1, mw2, mw3))):
        gate = jnp.dot(h, mw_ref[...], preferred_element_type=F32) + mb_ref[b:b + 1, :]
        proj = jnp.dot(y_ref[...], bw_ref[b], preferred_element_type=F32)
        term = jax.nn.sigmoid(gate) * proj
        acc = term if acc is None else acc + term
    o_ref[...] = acc.astype(o_ref.dtype)


def _merge(h, ys, merge_w, merge_b, branch_w, layer):
    t, d = h.shape
    width = BRANCH_WIDTH
    tm, tn = 1024, 256
    n_j = d // tn
    mw_spec = lambda b: pl.BlockSpec((None, d, tn), lambda i, j: (layer, 0, b * n_j + j))
    y_spec = _single_buffered((tm, width), lambda i, j: (i, 0))
    vmem = (tm * d * 2 + 4 * tm * width * 2 + 2 * 4 * d * tn * 2 + 2 * 4 * width * tn * 2
            + 6 * tm * tn * 4 + VMEM_SLACK)
    return pl.pallas_call(
        _merge_kernel,
        grid=(t // tm, n_j),
        in_specs=[_single_buffered((tm, d), lambda i, j: (i, 0)),
                  y_spec, y_spec, y_spec, y_spec,
                  mw_spec(0), mw_spec(1), mw_spec(2), mw_spec(3),
                  pl.BlockSpec((None, 4, tn), lambda i, j: (layer, 0, j)),
                  pl.BlockSpec((None, 4, width, tn), lambda i, j: (layer, 0, 0, j))],
        out_specs=pl.BlockSpec((tm, tn), lambda i, j: (i, j)),
        out_shape=jax.ShapeDtypeStruct((t, d), BF16),
        compiler_params=_params(2, vmem),
        name="merge",
    )(h, *ys, merge_w, merge_w, merge_w, merge_w, merge_b, branch_w)


def _ffn_up_kernel(h_ref, wg_ref, wu_ref, cw_ref, cb_ref, o_ref):
    h = h_ref[...]
    g = jnp.dot(h, wg_ref[...], preferred_element_type=F32)
    row = lax.broadcasted_iota(jnp.int32, g.shape, 0)
    g1 = jnp.where(row >= 1, pltpu.roll(g, 1, 0), 0.0)
    g2 = jnp.where(row >= 2, pltpu.roll(g, 2, 0), 0.0)
    a = cw_ref[2:3, :] * g + cw_ref[1:2, :] * g1 + cw_ref[0:1, :] * g2 + cb_ref[...]
    u = jnp.dot(h, wu_ref[...], preferred_element_type=F32)
    o_ref[...] = (jax.nn.gelu(a) * u).astype(o_ref.dtype)


def _ffn_up(h2, wg, wu, conv_w, conv_b, layer, bsz, seq):
    t, d = h2.shape
    f = wg.shape[2]
    tn = 256
    w_spec = pl.BlockSpec((None, d, tn), lambda b, j: (layer, 0, j))
    vmem = seq * d * 2 + 2 * 2 * d * tn * 2 + 8 * seq * tn * 4 + VMEM_SLACK
    return pl.pallas_call(
        _ffn_up_kernel,
        grid=(bsz, f // tn),
        in_specs=[_single_buffered((seq, d), lambda b, j: (b, 0)),
                  w_spec, w_spec,
                  pl.BlockSpec((None, 3, tn), lambda b, j: (layer, 0, j)),
                  pl.BlockSpec((None, 1, tn), lambda b, j: (layer, 0, j))],
        out_specs=pl.BlockSpec((seq, tn), lambda b, j: (b, j)),
        out_shape=jax.ShapeDtypeStruct((t, f), BF16),
        compiler_params=_params(2, vmem),
        name="ffn_up",
    )(h2, wg, wu, conv_w, conv_b)


def kernel(x, c, norm1_g, w_ada, b_ada, w_in, sgu_norm_g, sgu_w, sgu_b, pool_w, pool_scale, merge_w,
           merge_b, branch_w, out_w, norm2_g, ffn_wg, ffn_wu, conv_w, conv_b, ffn_wd, final_g):
    bsz, seq, d = x.shape
    depth = w_in.shape[0]
    width = BRANCH_WIDTH
    t = bsz * seq
    assert seq % 512 == 0 and seq // DIL_DILATIONS[-1] >= DIL_BLOCK and d == 4 * width

    w_in_b = w_in.astype(BF16)
    merge_w_b = merge_w.reshape(depth, d, 4 * d).astype(BF16)
    branch_w_b = branch_w.astype(BF16)
    out_w_b = out_w.astype(BF16)
    ffn_wg_b = ffn_wg.astype(BF16)
    ffn_wu_b = ffn_wu.astype(BF16)
    ffn_wd_b = ffn_wd.astype(BF16)
    pool_w_b = pool_w.astype(BF16)
    sgu_b_t = sgu_b.transpose(0, 2, 1)
    c_pad = jnp.zeros((8, d), BF16).at[:bsz].set(c.astype(BF16))
    row3 = lambda a: a.reshape(depth, 1, a.shape[-1])

    xf = x.reshape(t, d)
    for layer in range(depth):
        mod = _ada_mod(c_pad, w_ada, row3(b_ada), layer)[:bsz].reshape(bsz * N_MOD, 1, d)

        h = _norm_mod(xf, row3(norm1_g), layer, mod, 0, 1, seq)
        z_ab = _matmul(h, w_in_b, layer, 0, 3 * width, F32, 1024, 512)
        z_cd = _matmul(h, w_in_b, layer, 3 * width, 12 * width, BF16, 1024, 512)
        y_a = _sgu(z_ab, row3(sgu_norm_g), sgu_w, sgu_b_t, layer)
        y_b = _pool(z_ab, pool_w_b, row3(pool_scale), layer, bsz, seq)
        y_c = _dilated_mixer(z_cd, bsz, seq)
        y_d = _moba_mixer(z_cd, bsz, seq)
        merged = _merge(h, (y_a, y_b, y_c, y_d), merge_w_b, merge_b, branch_w_b, layer)
        xf = _matmul_residual(merged, out_w_b, layer, xf, mod, 2, seq, 1024, 512)

        h2 = _norm_mod(xf, row3(norm2_g), layer, mod, 3, 4, seq)
        f = _ffn_up(h2, ffn_wg_b, ffn_wu_b, conv_w, row3(conv_b), layer, bsz, seq)
        xf = _matmul_residual(f, ffn_wd_b, layer, xf, mod, 5, seq, 1024, 256)

    return _final_norm(xf, final_g.reshape(1, d)).reshape(bsz, seq, d)
```

```python
import functools

import jax
import jax.numpy as jnp
import numpy as np
from jax import lax
from jax.experimental import pallas as pl
from jax.experimental.pallas import tpu as pltpu

BF16 = jnp.bfloat16
F32 = jnp.float32

LANES = 128
HEAD_DIM = 128
N_HEADS = 8
BRANCH_WIDTH = N_HEADS * HEAD_DIM
SGU_CHUNK = 128
SGU_GROUPS = 8
POOL_WINDOWS = (2, 4, 8, 16)
POOL_GROUP_WIDTH = BRANCH_WIDTH // len(POOL_WINDOWS)
DIL_DILATIONS = (1, 4, 16)
DIL_BLOCK = 128
MOBA_BLOCK = 256
MOBA_TOPK = 3
N_ALIBI_HEADS = 32
ALIBI_SETS = 4
N_MOD = 6
NORM_EPS = 1e-6
NEG_INF = -1e30
ATTN_SCALE = HEAD_DIM ** -0.5

V7X_VMEM_BYTES = 64 * 2 ** 20
VMEM_LIMIT_CAP = 56 * 2 ** 20
MIB = 2 ** 20
VMEM_SLACK = 8 * MIB


def _alibi_slopes():
    n = N_ALIBI_HEADS
    return np.asarray(2.0 ** (-8.0 * np.arange(1, n + 1) / n), dtype=np.float32)


def _params(n_axes, vmem_bytes):
    limit = min(VMEM_LIMIT_CAP, max(32 * MIB, int(vmem_bytes)))
    return pltpu.CompilerParams(dimension_semantics=("arbitrary",) * n_axes,
                                vmem_limit_bytes=limit)


def _single_buffered(block_shape, index_map):
    return pl.BlockSpec(block_shape, index_map, pipeline_mode=pl.Buffered(1))


def _ada_kernel(c_ref, w_ref, b_ref, o_ref):
    w = w_ref[...].astype(BF16)
    o_ref[...] = jnp.dot(c_ref[...], w, preferred_element_type=F32) + b_ref[...]


def _ada_mod(c_pad, w_ada, b_ada, layer):
    d = c_pad.shape[1]
    n = w_ada.shape[2]
    tn = 512
    return pl.pallas_call(
        _ada_kernel,
        grid=(n // tn,),
        in_specs=[pl.BlockSpec((8, d), lambda j: (0, 0)),
                  pl.BlockSpec((None, d, tn), lambda j: (layer, 0, j)),
                  pl.BlockSpec((None, 1, tn), lambda j: (layer, 0, j))],
        out_specs=pl.BlockSpec((8, tn), lambda j: (0, j)),
        out_shape=jax.ShapeDtypeStruct((8, n), F32),
        compiler_params=_params(1, 2 * d * tn * 4 + 3 * d * tn * 2 + 8 * MIB),
        name="ada_mod",
    )(c_pad, w_ada, b_ada)


def _norm_mod_kernel(x_ref, g_ref, sc_ref, sh_ref, o_ref):
    x = x_ref[...]
    y = x * lax.rsqrt(jnp.mean(x * x, axis=-1, keepdims=True) + NORM_EPS) * g_ref[...]
    o_ref[...] = (y * (1.0 + sc_ref[...]) + sh_ref[...]).astype(o_ref.dtype)


def _norm_mod_residue_kernel(x_ref, g_ref, sc_ref, sh_ref, o_ref, *rest):
    perm_refs, hbuf = rest[:-1], rest[-1]
    x = x_ref[...]
    y = x * lax.rsqrt(jnp.mean(x * x, axis=-1, keepdims=True) + NORM_EPS) * g_ref[...]
    h = y * (1.0 + sc_ref[...]) + sh_ref[...]
    o_ref[...] = h.astype(o_ref.dtype)
    for c in range(hbuf.shape[0]):
        cols = slice(c * LANES, (c + 1) * LANES)
        hbuf[c] = h[:, cols]
        for p_ref in perm_refs:
            d, n = p_ref.shape[0], p_ref.shape[1]
            for r in range(d):
                p_ref[r, :, cols] = hbuf[c, pl.ds(r, n, stride=d), :].astype(p_ref.dtype)


def _norm_kernel(x_ref, g_ref, o_ref):
    x = x_ref[...]
    y = x * lax.rsqrt(jnp.mean(x * x, axis=-1, keepdims=True) + NORM_EPS) * g_ref[...]
    o_ref[...] = y.astype(o_ref.dtype)


def _norm_mod(x, g, layer, mod, shift_idx, scale_idx, seq, dilations=()):
    t, d = x.shape
    tm = 256
    per_seq = seq // tm
    bsz = t // seq
    in_specs = [pl.BlockSpec((tm, d), lambda i: (i, 0)),
                pl.BlockSpec((None, 1, d), lambda i: (layer, 0, 0)),
                pl.BlockSpec((None, 1, d), lambda i: ((i // per_seq) * N_MOD + scale_idx, 0, 0)),
                pl.BlockSpec((None, 1, d), lambda i: ((i // per_seq) * N_MOD + shift_idx, 0, 0))]
    main_spec = pl.BlockSpec((tm, d), lambda i: (i, 0))
    main_shape = jax.ShapeDtypeStruct((t, d), BF16)
    if not dilations:
        return pl.pallas_call(
            _norm_mod_kernel, grid=(t // tm,), in_specs=in_specs, out_specs=main_spec,
            out_shape=main_shape, compiler_params=_params(1, 6 * tm * d * 4 + VMEM_SLACK),
            name="norm_mod",
        )(x, g, mod, mod)
    perm_specs = [pl.BlockSpec((None, dd, tm // dd, d), lambda i: (i // per_seq, 0, i % per_seq, 0))
                  for dd in dilations]
    perm_shapes = [jax.ShapeDtypeStruct((bsz, dd, seq // dd, d), BF16) for dd in dilations]
    return pl.pallas_call(
        _norm_mod_residue_kernel, grid=(t // tm,), in_specs=in_specs,
        out_specs=[main_spec] + perm_specs, out_shape=[main_shape] + perm_shapes,
        scratch_shapes=[pltpu.VMEM((d // LANES, tm, LANES), F32)],
        compiler_params=_params(1, (7 + 2 * len(dilations)) * tm * d * 4 + VMEM_SLACK),
        name="norm_mod_residue",
    )(x, g, mod, mod)


def _final_norm(x, g):
    t, d = x.shape
    tm = 256
    return pl.pallas_call(
        _norm_kernel,
        grid=(t // tm,),
        in_specs=[pl.BlockSpec((tm, d), lambda i: (i, 0)),
                  pl.BlockSpec((1, d), lambda i: (0, 0))],
        out_specs=pl.BlockSpec((tm, d), lambda i: (i, 0)),
        out_shape=jax.ShapeDtypeStruct((t, d), F32),
        compiler_params=_params(1, 6 * tm * d * 4 + 8 * MIB),
        name="final_norm",
    )(x, g)


def _mm_kernel(a_ref, w_ref, o_ref):
    o_ref[...] = jnp.dot(a_ref[...], w_ref[...], preferred_element_type=F32).astype(o_ref.dtype)


def _mm_residual_kernel(a_ref, w_ref, x_ref, g_ref, o_ref):
    acc = jnp.dot(a_ref[...], w_ref[...], preferred_element_type=F32)
    o_ref[...] = x_ref[...] + g_ref[...] * acc


def _matmul(a, w, layer, col0, n_cols, out_dtype, tm, tn, run=None, run_stride=None):
    m, k = a.shape
    run = n_cols if run is None else run
    run_stride = run if run_stride is None else run_stride
    assert col0 % tn == 0 and run % tn == 0 and run_stride % tn == 0 and n_cols % run == 0
    per_run = run // tn
    col_block = lambda j: (col0 + (j // per_run) * run_stride) // tn + j % per_run
    vmem = 2 * tm * k * 2 + 2 * k * tn * 2 + 3 * tm * tn * 4 + VMEM_SLACK
    return pl.pallas_call(
        _mm_kernel,
        grid=(m // tm, n_cols // tn),
        in_specs=[pl.BlockSpec((tm, k), lambda i, j: (i, 0)),
                  pl.BlockSpec((None, k, tn), lambda i, j: (layer, 0, col_block(j)))],
        out_specs=pl.BlockSpec((tm, tn), lambda i, j: (i, j)),
        out_shape=jax.ShapeDtypeStruct((m, n_cols), out_dtype),
        compiler_params=_params(2, vmem),
        name="proj",
    )(a, w)


def _matmul_residual(a, w, layer, x, mod, gate_idx, seq, tm, tn):
    m, k = a.shape
    n = w.shape[2]
    per_seq = seq // tm
    vmem = tm * k * 2 + 2 * k * tn * 2 + 5 * tm * tn * 4 + VMEM_SLACK
    return pl.pallas_call(
        _mm_residual_kernel,
        grid=(m // tm, n // tn),
        in_specs=[_single_buffered((tm, k), lambda i, j: (i, 0)),
                  pl.BlockSpec((None, k, tn), lambda i, j: (layer, 0, j)),
                  pl.BlockSpec((tm, tn), lambda i, j: (i, j)),
                  pl.BlockSpec((None, 1, tn), lambda i, j: ((i // per_seq) * N_MOD + gate_idx, 0, j))],
        out_specs=pl.BlockSpec((tm, tn), lambda i, j: (i, j)),
        out_shape=jax.ShapeDtypeStruct((m, n), F32),
        compiler_params=_params(2, vmem),
        name="proj_residual",
    )(a, w, x, mod)


def _sgu_kernel(z_ref, g_ref, w_ref, bt_ref, o_ref, *, n_chunk):
    width = o_ref.shape[1]
    gw = width // SGU_GROUPS
    ti = lax.broadcasted_iota(jnp.int32, (SGU_CHUNK, SGU_CHUNK), 0)
    si = lax.broadcasted_iota(jnp.int32, (SGU_CHUNK, SGU_CHUNK), 1)
    causal = ti >= si
    w_c = [jnp.where(causal, w_ref[g], 0.0).astype(BF16) for g in range(SGU_GROUPS)]
    for c in range(n_chunk):
        rows = slice(c * SGU_CHUNK, (c + 1) * SGU_CHUNK)
        ge = jax.nn.gelu(z_ref[rows, :])
        u = ge[:, :width]
        v = ge[:, width:]
        v = v * lax.rsqrt(jnp.mean(v * v, axis=-1, keepdims=True) + NORM_EPS) * g_ref[...]
        vb = v.astype(BF16)
        for g in range(SGU_GROUPS):
            cols = slice(g * gw, (g + 1) * gw)
            mixed = jnp.dot(w_c[g], vb[:, cols], preferred_element_type=F32) + bt_ref[:, g:g + 1]
            o_ref[rows, cols] = (u[:, cols] * mixed).astype(o_ref.dtype)


def _sgu(z_ab, norm_g, w_s, b_t, layer):
    t = z_ab.shape[0]
    width = BRANCH_WIDTH
    n_chunk = 4
    tm = n_chunk * SGU_CHUNK
    return pl.pallas_call(
        functools.partial(_sgu_kernel, n_chunk=n_chunk),
        grid=(t // tm,),
        in_specs=[pl.BlockSpec((tm, 2 * width), lambda i: (i, 0)),
                  pl.BlockSpec((None, 1, width), lambda i: (layer, 0, 0)),
                  pl.BlockSpec((None, SGU_GROUPS, SGU_CHUNK, SGU_CHUNK), lambda i: (layer, 0, 0, 0)),
                  pl.BlockSpec((None, SGU_CHUNK, SGU_GROUPS), lambda i: (layer, 0, 0))],
        out_specs=pl.BlockSpec((tm, width), lambda i: (i, 0)),
        out_shape=jax.ShapeDtypeStruct((t, width), BF16),
        compiler_params=_params(1, 32 * MIB),
        name="sgu",
    )(z_ab, norm_g, w_s, b_t)


def _pool_kernel(z_ref, w_ref, sc_ref, o_ref):
    seq = z_ref.shape[0]
    pw = POOL_GROUP_WIDTH
    row = lax.broadcasted_iota(jnp.int32, (seq, pw), 0)
    pos1 = (lax.broadcasted_iota(jnp.int32, (seq, 1), 0) + 1).astype(F32)
    for g, window in enumerate(POOL_WINDOWS):
        cols = slice(g * pw, (g + 1) * pw)
        z = z_ref[:, cols]
        s = z
        shift = 1
        while shift < window:
            s = s + jnp.where(row >= shift, pltpu.roll(s, shift, 0), 0.0)
            shift *= 2
        count = jnp.minimum(pos1, float(window))
        mixed = (s / count - z).astype(BF16)
        y = jnp.dot(mixed, w_ref[g], preferred_element_type=F32) * sc_ref[:, cols]
        o_ref[:, cols] = y.astype(o_ref.dtype)


def _pool(z_ab, w_pool, scale, layer, bsz, seq):
    t = z_ab.shape[0]
    width = BRANCH_WIDTH
    pw = POOL_GROUP_WIDTH
    return pl.pallas_call(
        _pool_kernel,
        grid=(bsz,),
        in_specs=[pl.BlockSpec((seq, width), lambda b: (b, 2)),
                  pl.BlockSpec((None, len(POOL_WINDOWS), pw, pw), lambda b: (layer, 0, 0, 0)),
                  pl.BlockSpec((None, 1, width), lambda b: (layer, 0, 0))],
        out_specs=pl.BlockSpec((seq, width), lambda b: (b, 0)),
        out_shape=jax.ShapeDtypeStruct((t, width), BF16),
        compiler_params=_params(1, 48 * MIB),
        name="pool",
    )(z_ab, w_pool, scale)


def _dilated_kernel(q_ref, k_ref, v_ref, kp_ref, vp_ref, o_ref, lse_ref, kbuf, vbuf, *,
                    slopes, n_seq, q_rows, dilation):
    blk = DIL_BLOCK
    n_blk = q_rows // blk
    chunk = pl.program_id(1)
    first_residue = (pl.program_id(0) * n_seq) % dilation
    qi = lax.broadcasted_iota(jnp.int32, (blk, 2 * blk), 0)
    kk = lax.broadcasted_iota(jnp.int32, (blk, 2 * blk), 1)
    steps = qi - kk + blk
    in_window = (steps >= 0) & (steps <= blk)
    steps_f = steps.astype(F32)
    lane = lax.broadcasted_iota(jnp.int32, (blk, HEAD_DIM), 1)
    for s in range(n_seq):
        kbuf[0:blk, :] = kp_ref[s]
        kbuf[blk:, :] = k_ref[s]
        vbuf[0:blk, :] = vp_ref[s]
        vbuf[blk:, :] = v_ref[s]

        def block_body(j, carry, s=s):
            r0 = pl.multiple_of(j * blk, blk)
            key_sub = (chunk * n_blk + j - 1) * blk + kk
            valid = in_window & (key_sub >= 0)
            lse_tile = jnp.zeros((blk, HEAD_DIM), F32)
            if dilation == 1:
                out_rows = pl.ds(r0, blk)
            else:
                start = (chunk * q_rows + r0) * dilation + first_residue + s
                out_rows = pl.ds(start, blk, stride=dilation)
            for h in range(N_HEADS):
                cols = slice(h * HEAD_DIM, (h + 1) * HEAD_DIM)
                q = q_ref[s, pl.ds(r0, blk), cols]
                k = kbuf[pl.ds(r0, 2 * blk), cols]
                v = vbuf[pl.ds(r0, 2 * blk), cols]
                sc = lax.dot_general(q, k, (((1,), (1,)), ((), ())),
                                     preferred_element_type=F32) * ATTN_SCALE
                sc = jnp.where(valid, sc - slopes[h] * steps_f, NEG_INF)
                m = jnp.max(sc, axis=-1, keepdims=True)
                p = jnp.exp(sc - m)
                den = jnp.sum(p, axis=-1, keepdims=True)
                o = jnp.dot(p.astype(BF16), v, preferred_element_type=F32) / den
                o_ref[h, out_rows, :] = o
                lse_tile = jnp.where(lane == h, m + jnp.log(den), lse_tile)
            lse_ref[out_rows, :] = lse_tile
            return carry

        lax.fori_loop(0, n_blk, block_body, 0)


def _dilated_attention(z_g, dilation, group, bsz, seq):
    width = BRANCH_WIDTH
    sub = seq // dilation
    n_sub_seq = bsz * dilation
    arr = z_g.reshape(n_sub_seq, sub, 3 * width)
    q_rows = min(sub, 512)
    n_seq = min(max(1, 1024 // sub), dilation)
    assert dilation % n_seq == 0 and sub % q_rows == 0
    assert dilation == 1 or q_rows == sub
    blocks_per_chunk = q_rows // DIL_BLOCK
    slopes = tuple(float(s) * dilation for s in _alibi_slopes()[group::ALIBI_SETS])

    def prev_block(c):
        return jnp.maximum(c * blocks_per_chunk - 1, 0)

    if dilation == 1:
        assert n_seq == 1
        out_specs = [pl.BlockSpec((None, N_HEADS, q_rows, HEAD_DIM), lambda s, c: (s, 0, c, 0)),
                     pl.BlockSpec((None, q_rows, LANES), lambda s, c: (s, c, 0))]
    else:
        out_specs = [pl.BlockSpec((None, N_HEADS, seq, HEAD_DIM),
                                  lambda s, c: (s * n_seq // dilation, 0, 0, 0)),
                     pl.BlockSpec((None, seq, LANES), lambda s, c: (s * n_seq // dilation, 0, 0))]
    return pl.pallas_call(
        functools.partial(_dilated_kernel, slopes=slopes, n_seq=n_seq, q_rows=q_rows, dilation=dilation),
        grid=(n_sub_seq // n_seq, sub // q_rows),
        in_specs=[pl.BlockSpec((n_seq, q_rows, width), lambda s, c: (s, c, 0)),
                  pl.BlockSpec((n_seq, q_rows, width), lambda s, c: (s, c, 1)),
                  pl.BlockSpec((n_seq, q_rows, width), lambda s, c: (s, c, 2)),
                  pl.BlockSpec((n_seq, DIL_BLOCK, width), lambda s, c: (s, prev_block(c), 1)),
                  pl.BlockSpec((n_seq, DIL_BLOCK, width), lambda s, c: (s, prev_block(c), 2))],
        out_specs=out_specs,
        out_shape=[jax.ShapeDtypeStruct((bsz, N_HEADS, seq, HEAD_DIM), F32),
                   jax.ShapeDtypeStruct((bsz, seq, LANES), F32)],
        scratch_shapes=[pltpu.VMEM((DIL_BLOCK + q_rows, width), BF16),
                        pltpu.VMEM((DIL_BLOCK + q_rows, width), BF16)],
        compiler_params=_params(2, 48 * MIB),
        name="dilated_attn",
    )(arr, arr, arr, arr, arr)


def _dilated_combine_kernel(o0, o1, o2, l0, l1, l2, y_ref):
    for h in range(N_HEADS):
        cols = slice(h * HEAD_DIM, (h + 1) * HEAD_DIM)
        a0 = l0[:, h:h + 1]
        a1 = l1[:, h:h + 1]
        a2 = l2[:, h:h + 1]
        m = jnp.maximum(jnp.maximum(a0, a1), a2)
        e0 = jnp.exp(a0 - m)
        e1 = jnp.exp(a1 - m)
        e2 = jnp.exp(a2 - m)
        y = (e0 * o0[h] + e1 * o1[h] + e2 * o2[h]) / (e0 + e1 + e2)
        y_ref[:, cols] = y.astype(y_ref.dtype)


def _dilated_combine(outs, lses):
    bsz, _, seq, _ = outs[0].shape
    width = BRANCH_WIDTH
    tm = 512
    per_seq = seq // tm
    o_spec = pl.BlockSpec((None, N_HEADS, tm, HEAD_DIM), lambda i: (i // per_seq, 0, i % per_seq, 0))
    l_spec = pl.BlockSpec((None, tm, LANES), lambda i: (i // per_seq, i % per_seq, 0))
    return pl.pallas_call(
        _dilated_combine_kernel,
        grid=(bsz * per_seq,),
        in_specs=[o_spec, o_spec, o_spec, l_spec, l_spec, l_spec],
        out_specs=pl.BlockSpec((tm, width), lambda i: (i, 0)),
        out_shape=jax.ShapeDtypeStruct((bsz * seq, width), BF16),
        compiler_params=_params(1, 32 * MIB),
        name="dilated_combine",
    )(*outs, *lses)


def _dilated_mixer(z_groups, bsz, seq):
    results = [_dilated_attention(z_groups[g], d, g, bsz, seq) for g, d in enumerate(DIL_DILATIONS)]
    return _dilated_combine([o for o, _ in results], [lse for _, lse in results])


def _moba_kernel(slopes_ref, q_ref, k_ref, v_ref, o_ref):
    blk = MOBA_BLOCK
    seq = k_ref.shape[0]
    n_blk = seq // blk
    slope = slopes_ref[pl.program_id(1)]

    k_mean = jnp.sum(k_ref[...].astype(F32).reshape(n_blk, blk, HEAD_DIM), axis=1) * (1.0 / blk)
    k_mean = jnp.concatenate([k_mean, jnp.zeros((HEAD_DIM - n_blk, HEAD_DIM), F32)], axis=0)
    gate_all = lax.dot_general(q_ref[...].astype(F32), k_mean, (((1,), (1,)), ((), ())),
                               precision=lax.Precision.HIGHEST, preferred_element_type=F32)
    lane = lax.broadcasted_iota(jnp.int32, (blk, HEAD_DIM), 1)
    qi = lax.broadcasted_iota(jnp.int32, (blk, blk), 0)
    ki = lax.broadcasted_iota(jnp.int32, (blk, blk), 1)
    causal = qi >= ki
    bias0 = slope * (qi - ki).astype(F32)

    for i in range(n_blk):
        rows = slice(i * blk, (i + 1) * blk)
        q = q_ref[rows, :]
        if i > MOBA_TOPK:
            gate = jnp.where(lane < i, gate_all[rows, :], NEG_INF)
            rank = jnp.zeros((blk, HEAD_DIM), jnp.int32)
            for m in range(i):
                col = gate[:, m:m + 1]
                ahead = (col > gate) | ((col == gate) & (lane > m))
                rank = rank + ahead.astype(jnp.int32)
        scores = []
        for n in range(i + 1):
            kn = k_ref[n * blk:(n + 1) * blk, :]
            s = lax.dot_general(q, kn, (((1,), (1,)), ((), ())), preferred_element_type=F32) * ATTN_SCALE
            s = s - (bias0 + slope * float((i - n) * blk))
            if n == i:
                s = jnp.where(causal, s, NEG_INF)
            elif i > MOBA_TOPK:
                s = jnp.where(rank[:, n:n + 1] < MOBA_TOPK, s, NEG_INF)
            scores.append(s)
        m_row = functools.reduce(jnp.maximum, [jnp.max(s, axis=-1, keepdims=True) for s in scores])
        den = jnp.zeros((blk, 1), F32)
        acc = jnp.zeros((blk, HEAD_DIM), F32)
        for n, s in enumerate(scores):
            p = jnp.exp(s - m_row)
            den = den + jnp.sum(p, axis=-1, keepdims=True)
            acc = acc + jnp.dot(p.astype(BF16), v_ref[n * blk:(n + 1) * blk, :],
                                preferred_element_type=F32)
        o_ref[rows, :] = (acc / den).astype(o_ref.dtype)


def _moba_mixer(z_d, bsz, seq):
    width = BRANCH_WIDTH
    z3 = z_d.reshape(bsz, seq, 3 * width)
    slopes = jnp.asarray(_alibi_slopes()[len(DIL_DILATIONS)::ALIBI_SETS])
    spec = lambda off: pl.BlockSpec((None, seq, HEAD_DIM), lambda b, h: (b, 0, off + h))
    out = pl.pallas_call(
        _moba_kernel,
        grid=(bsz, N_HEADS),
        in_specs=[pl.BlockSpec(memory_space=pltpu.SMEM), spec(0), spec(N_HEADS), spec(2 * N_HEADS)],
        out_specs=spec(0),
        out_shape=jax.ShapeDtypeStruct((bsz, seq, width), BF16),
        compiler_params=_params(2, 32 * MIB),
        name="moba",
    )(slopes, z3, z3, z3)
    return out.reshape(bsz * seq, width)


def _merge_kernel(h_ref, ya_ref, yb_ref, yc_ref, yd_ref, mw0, mw1, mw2, mw3, mb_ref, bw_ref, o_ref):
    h = h_ref[...]
    acc = None
    for b, (y_ref, mw_ref) in enumerate(zip((ya_ref, yb_ref, yc_ref, yd_ref), (mw0, mw1, mw2, mw3))):
        gate = jnp.dot(h, mw_ref[...], preferred_element_type=F32) + mb_ref[b:b + 1, :]
        proj = jnp.dot(y_ref[...], bw_ref[b], preferred_element_type=F32)
        term = jax.nn.sigmoid(gate) * proj
        acc = term if acc is None else acc + term
    o_ref[...] = acc.astype(o_ref.dtype)


def _merge(h, ys, merge_w, merge_b, branch_w, layer):
    t, d = h.shape
    width = BRANCH_WIDTH
    tm, tn = 1024, 256
    n_j = d // tn
    mw_spec = lambda b: pl.BlockSpec((None, d, tn), lambda i, j: (layer, 0, b * n_j + j))
    y_spec = _single_buffered((tm, width), lambda i, j: (i, 0))
    vmem = (tm * d * 2 + 4 * tm * width * 2 + 2 * 4 * d * tn * 2 + 2 * 4 * width * tn * 2
            + 6 * tm * tn * 4 + VMEM_SLACK)
    return pl.pallas_call(
        _merge_kernel,
        grid=(t // tm, n_j),
        in_specs=[_single_buffered((tm, d), lambda i, j: (i, 0)),
                  y_spec, y_spec, y_spec, y_spec,
                  mw_spec(0), mw_spec(1), mw_spec(2), mw_spec(3),
                  pl.BlockSpec((None, 4, tn), lambda i, j: (layer, 0, j)),
                  pl.BlockSpec((None, 4, width, tn), lambda i, j: (layer, 0, 0, j))],
        out_specs=pl.BlockSpec((tm, tn), lambda i, j: (i, j)),
        out_shape=jax.ShapeDtypeStruct((t, d), BF16),
        compiler_params=_params(2, vmem),
        name="merge",
    )(h, *ys, merge_w, merge_w, merge_w, merge_w, merge_b, branch_w)


def _ffn_up_kernel(h_ref, wg_ref, wu_ref, cw_ref, cb_ref, o_ref):
    h = h_ref[...]
    g = jnp.dot(h, wg_ref[...], preferred_element_type=F32)
    row = lax.broadcasted_iota(jnp.int32, g.shape, 0)
    g1 = jnp.where(row >= 1, pltpu.roll(g, 1, 0), 0.0)
    g2 = jnp.where(row >= 2, pltpu.roll(g, 2, 0), 0.0)
    a = cw_ref[2:3, :] * g + cw_ref[1:2, :] * g1 + cw_ref[0:1, :] * g2 + cb_ref[...]
    u = jnp.dot(h, wu_ref[...], preferred_element_type=F32)
    o_ref[...] = (jax.nn.gelu(a) * u).astype(o_ref.dtype)


def _ffn_up(h2, wg, wu, conv_w, conv_b, layer, bsz, seq):
    t, d = h2.shape
    f = wg.shape[2]
    tn = 256
    w_spec = pl.BlockSpec((None, d, tn), lambda b, j: (layer, 0, j))
    vmem = seq * d * 2 + 2 * 2 * d * tn * 2 + 8 * seq * tn * 4 + VMEM_SLACK
    return pl.pallas_call(
        _ffn_up_kernel,
        grid=(bsz, f // tn),
        in_specs=[_single_buffered((seq, d), lambda b, j: (b, 0)),
                  w_spec, w_spec,
                  pl.BlockSpec((None, 3, tn), lambda b, j: (layer, 0, j)),
                  pl.BlockSpec((None, 1, tn), lambda b, j: (layer, 0, j))],
        out_specs=pl.BlockSpec((seq, tn), lambda b, j: (b, j)),
        out_shape=jax.ShapeDtypeStruct((t, f), BF16),
        compiler_params=_params(2, vmem),
        name="ffn_up",
    )(h2, wg, wu, conv_w, conv_b)


def kernel(x, c, norm1_g, w_ada, b_ada, w_in, sgu_norm_g, sgu_w, sgu_b, pool_w, pool_scale, merge_w,
           merge_b, branch_w, out_w, norm2_g, ffn_wg, ffn_wu, conv_w, conv_b, ffn_wd, final_g):
    bsz, seq, d = x.shape
    depth = w_in.shape[0]
    width = BRANCH_WIDTH
    t = bsz * seq
    assert seq % 512 == 0 and seq // DIL_DILATIONS[-1] >= DIL_BLOCK and d == 4 * width

    w_in_b = w_in.astype(BF16)
    merge_w_b = merge_w.reshape(depth, d, 4 * d).astype(BF16)
    branch_w_b = branch_w.astype(BF16)
    out_w_b = out_w.astype(BF16)
    ffn_wg_b = ffn_wg.astype(BF16)
    ffn_wu_b = ffn_wu.astype(BF16)
    ffn_wd_b = ffn_wd.astype(BF16)
    pool_w_b = pool_w.astype(BF16)
    sgu_b_t = sgu_b.transpose(0, 2, 1)
    c_pad = jnp.zeros((8, d), BF16).at[:bsz].set(c.astype(BF16))
    row3 = lambda a: a.reshape(depth, 1, a.shape[-1])

    xf = x.reshape(t, d)
    for layer in range(depth):
        mod = _ada_mod(c_pad, w_ada, row3(b_ada), layer)[:bsz].reshape(bsz * N_MOD, 1, d)

        h, *h_res = _norm_mod(xf, row3(norm1_g), layer, mod, 0, 1, seq, DIL_DILATIONS[1:])
        h_by_group = [h] + [hr.reshape(t, d) for hr in h_res]
        z_ab = _matmul(h, w_in_b, layer, 0, 3 * width, F32, 1024, 512)
        n_groups = len(DIL_DILATIONS)
        z_groups = [_matmul(h_by_group[g], w_in_b, layer, (3 + g) * width, 3 * width, BF16, 1024, 512,
                            run=width, run_stride=n_groups * width) for g in range(n_groups)]
        z_d = _matmul(h, w_in_b, layer, (3 + 3 * n_groups) * width, 3 * width, BF16, 1024, 512)
        y_a = _sgu(z_ab, row3(sgu_norm_g), sgu_w, sgu_b_t, layer)
        y_b = _pool(z_ab, pool_w_b, row3(pool_scale), layer, bsz, seq)
        y_c = _dilated_mixer(z_groups, bsz, seq)
        y_d = _moba_mixer(z_d, bsz, seq)
        merged = _merge(h, (y_a, y_b, y_c, y_d), merge_w_b, merge_b, branch_w_b, layer)
        xf = _matmul_residual(merged, out_w_b, layer, xf, mod, 2, seq, 1024, 512)

        h2 = _norm_mod(xf, row3(norm2_g), layer, mod, 3, 4, seq)
        f = _ffn_up(h2, ffn_wg_b, ffn_wu_b, conv_w, row3(conv_b), layer, bsz, seq)
        xf = _matmul_residual(f, ffn_wd_b, layer, xf, mod, 5, seq, 1024, 256)

    return _final_norm(xf, final_g.reshape(1, d)).reshape(bsz, seq, d)
```

```python
import functools

import jax
import jax.numpy as jnp
import numpy as np
from jax import lax
from jax.experimental import pallas as pl
from jax.experimental.pallas import tpu as pltpu

BF16 = jnp.bfloat16
F32 = jnp.float32

LANES = 128
HEAD_DIM = 128
N_HEADS = 8
BRANCH_WIDTH = N_HEADS * HEAD_DIM
SGU_CHUNK = 128
SGU_GROUPS = 8
POOL_WINDOWS = (2, 4, 8, 16)
POOL_GROUP_WIDTH = BRANCH_WIDTH // len(POOL_WINDOWS)
DIL_DILATIONS = (1, 4, 16)
DIL_BLOCK = 128
MOBA_BLOCK = 256
MOBA_TOPK = 3
N_ALIBI_HEADS = 32
ALIBI_SETS = 4
N_MOD = 6
NORM_EPS = 1e-6
NEG_INF = -1e30
ATTN_SCALE = HEAD_DIM ** -0.5
FFN_ROW_SPLIT = 1024

V7X_VMEM_BYTES = 64 * 2 ** 20
VMEM_LIMIT_CAP = 56 * 2 ** 20
MIB = 2 ** 20
VMEM_SLACK = 8 * MIB


def _alibi_slopes():
    n = N_ALIBI_HEADS
    return np.asarray(2.0 ** (-8.0 * np.arange(1, n + 1) / n), dtype=np.float32)


def _params(n_axes, vmem_bytes):
    limit = min(VMEM_LIMIT_CAP, max(32 * MIB, int(vmem_bytes)))
    return pltpu.CompilerParams(dimension_semantics=("arbitrary",) * n_axes,
                                vmem_limit_bytes=limit)


def _single_buffered(block_shape, index_map):
    return pl.BlockSpec(block_shape, index_map, pipeline_mode=pl.Buffered(1))


def _divisors(n, multiple_of):
    return [v for v in range(multiple_of, n + 1, multiple_of) if n % v == 0]


def _cast_job(src, layer, grid):
    _, rows, cols = src.shape
    n_steps = int(np.prod(grid))
    choices = [(rb * cb, -cb, rb, cb) for rb in _divisors(rows, 16) for cb in _divisors(cols, LANES)
               if (rows // rb) * (cols // cb) <= n_steps]
    _, _, rb, cb = min(choices)
    n_cb = cols // cb
    n_blocks = (rows // rb) * n_cb

    def block(*idx):
        step = idx[0]
        for extent, i in zip(grid[1:], idx[1:]):
            step = step * extent + i
        step = jnp.minimum(step, n_blocks - 1)
        return step // n_cb, step % n_cb

    return dict(src=src,
                in_spec=pl.BlockSpec((None, rb, cb), lambda *idx: (layer,) + block(*idx)),
                out_spec=pl.BlockSpec((rb, cb), lambda *idx: block(*idx)),
                out_shape=jax.ShapeDtypeStruct((rows, cols), BF16),
                vmem=2 * rb * cb * (4 + 2))


def _hosted(body, n_in, n_out, n_cast):
    def kernel_body(*refs):
        ins, refs = refs[:n_in], refs[n_in:]
        srcs, refs = refs[:n_cast], refs[n_cast:]
        outs, refs = refs[:n_out], refs[n_out:]
        dsts, scratch = refs[:n_cast], refs[n_cast:]
        body(*ins, *outs, *scratch)
        for s_ref, d_ref in zip(srcs, dsts):
            d_ref[...] = s_ref[...].astype(d_ref.dtype)
    return kernel_body


def _host_call(body, grid, in_specs, out_specs, out_shapes, operands, casts, vmem, name, scratch_shapes=()):
    jobs = [_cast_job(src, layer, grid) for src, layer in casts]
    outs = pl.pallas_call(
        _hosted(body, len(in_specs), len(out_specs), len(jobs)),
        grid=grid,
        in_specs=list(in_specs) + [j["in_spec"] for j in jobs],
        out_specs=list(out_specs) + [j["out_spec"] for j in jobs],
        out_shape=list(out_shapes) + [j["out_shape"] for j in jobs],
        scratch_shapes=list(scratch_shapes),
        compiler_params=_params(len(grid), vmem + sum(j["vmem"] for j in jobs)),
        name=name,
    )(*operands, *[j["src"] for j in jobs])
    return outs[:len(out_specs)], outs[len(out_specs):]


def _ada_kernel(c_ref, w_ref, b_ref, o_ref):
    w = w_ref[...].astype(BF16)
    o_ref[...] = jnp.dot(c_ref[...], w, preferred_element_type=F32) + b_ref[...]


def _ada_mod(c_pad, w_ada, b_ada, layer):
    d = c_pad.shape[1]
    n = w_ada.shape[2]
    tn = 512
    return pl.pallas_call(
        _ada_kernel,
        grid=(n // tn,),
        in_specs=[pl.BlockSpec((8, d), lambda j: (0, 0)),
                  pl.BlockSpec((None, d, tn), lambda j: (layer, 0, j)),
                  pl.BlockSpec((None, 1, tn), lambda j: (layer, 0, j))],
        out_specs=pl.BlockSpec((8, tn), lambda j: (0, j)),
        out_shape=jax.ShapeDtypeStruct((8, n), F32),
        compiler_params=_params(1, 2 * d * tn * 4 + 3 * d * tn * 2 + 8 * MIB),
        name="ada_mod",
    )(c_pad, w_ada, b_ada)


def _norm_mod_kernel(x_ref, g_ref, sc_ref, sh_ref, o_ref):
    x = x_ref[...]
    y = x * lax.rsqrt(jnp.mean(x * x, axis=-1, keepdims=True) + NORM_EPS) * g_ref[...]
    o_ref[...] = (y * (1.0 + sc_ref[...]) + sh_ref[...]).astype(o_ref.dtype)


def _norm_mod_residue_kernel(x_ref, g_ref, sc_ref, sh_ref, o_ref, *rest):
    perm_refs, hbuf = rest[:-1], rest[-1]
    x = x_ref[...]
    y = x * lax.rsqrt(jnp.mean(x * x, axis=-1, keepdims=True) + NORM_EPS) * g_ref[...]
    h = y * (1.0 + sc_ref[...]) + sh_ref[...]
    o_ref[...] = h.astype(o_ref.dtype)
    for c in range(hbuf.shape[0]):
        cols = slice(c * LANES, (c + 1) * LANES)
        hbuf[c] = h[:, cols]
        for p_ref in perm_refs:
            d, n = p_ref.shape[0], p_ref.shape[1]
            for r in range(d):
                p_ref[r, :, cols] = hbuf[c, pl.ds(r, n, stride=d), :].astype(p_ref.dtype)


def _norm_kernel(x_ref, g_ref, o_ref):
    x = x_ref[...]
    y = x * lax.rsqrt(jnp.mean(x * x, axis=-1, keepdims=True) + NORM_EPS) * g_ref[...]
    o_ref[...] = y.astype(o_ref.dtype)


def _norm_mod(x, g, layer, mod, shift_idx, scale_idx, seq, dilations=()):
    t, d = x.shape
    tm = 256
    per_seq = seq // tm
    bsz = t // seq
    in_specs = [pl.BlockSpec((tm, d), lambda i: (i, 0)),
                pl.BlockSpec((None, 1, d), lambda i: (layer, 0, 0)),
                pl.BlockSpec((None, 1, d), lambda i: ((i // per_seq) * N_MOD + scale_idx, 0, 0)),
                pl.BlockSpec((None, 1, d), lambda i: ((i // per_seq) * N_MOD + shift_idx, 0, 0))]
    main_spec = pl.BlockSpec((tm, d), lambda i: (i, 0))
    main_shape = jax.ShapeDtypeStruct((t, d), BF16)
    if not dilations:
        return pl.pallas_call(
            _norm_mod_kernel, grid=(t // tm,), in_specs=in_specs, out_specs=main_spec,
            out_shape=main_shape, compiler_params=_params(1, 6 * tm * d * 4 + VMEM_SLACK),
            name="norm_mod",
        )(x, g, mod, mod)
    perm_specs = [pl.BlockSpec((None, dd, tm // dd, d), lambda i: (i // per_seq, 0, i % per_seq, 0))
                  for dd in dilations]
    perm_shapes = [jax.ShapeDtypeStruct((bsz, dd, seq // dd, d), BF16) for dd in dilations]
    return pl.pallas_call(
        _norm_mod_residue_kernel, grid=(t // tm,), in_specs=in_specs,
        out_specs=[main_spec] + perm_specs, out_shape=[main_shape] + perm_shapes,
        scratch_shapes=[pltpu.VMEM((d // LANES, tm, LANES), F32)],
        compiler_params=_params(1, (7 + 2 * len(dilations)) * tm * d * 4 + VMEM_SLACK),
        name="norm_mod_residue",
    )(x, g, mod, mod)


def _final_norm(x, g):
    t, d = x.shape
    tm = 256
    return pl.pallas_call(
        _norm_kernel,
        grid=(t // tm,),
        in_specs=[pl.BlockSpec((tm, d), lambda i: (i, 0)),
                  pl.BlockSpec((1, d), lambda i: (0, 0))],
        out_specs=pl.BlockSpec((tm, d), lambda i: (i, 0)),
        out_shape=jax.ShapeDtypeStruct((t, d), F32),
        compiler_params=_params(1, 6 * tm * d * 4 + 8 * MIB),
        name="final_norm",
    )(x, g)


def _mm_kernel(a_ref, w_ref, o_ref):
    o_ref[...] = jnp.dot(a_ref[...], w_ref[...], preferred_element_type=F32).astype(o_ref.dtype)


def _mm_residual_kernel(a_ref, w_ref, x_ref, g_ref, o_ref):
    acc = jnp.dot(a_ref[...], w_ref[...], preferred_element_type=F32)
    o_ref[...] = x_ref[...] + g_ref[...] * acc


def _matmul(a, w, col0, n_cols, out_dtype, tm, tn, run=None, run_stride=None, casts=()):
    m, k = a.shape
    run = n_cols if run is None else run
    run_stride = run if run_stride is None else run_stride
    assert col0 % tn == 0 and run % tn == 0 and run_stride % tn == 0 and n_cols % run == 0
    per_run = run // tn
    col_block = lambda j: (col0 + (j // per_run) * run_stride) // tn + j % per_run
    vmem = 2 * tm * k * 2 + 2 * k * tn * 2 + 3 * tm * tn * 4 + VMEM_SLACK
    (out,), cast_outs = _host_call(
        _mm_kernel, (m // tm, n_cols // tn),
        [pl.BlockSpec((tm, k), lambda i, j: (i, 0)),
         pl.BlockSpec((k, tn), lambda i, j: (0, col_block(j)))],
        [pl.BlockSpec((tm, tn), lambda i, j: (i, j))],
        [jax.ShapeDtypeStruct((m, n_cols), out_dtype)],
        (a, w), casts, vmem, "proj")
    return out, cast_outs


def _matmul_residual(a, w, x, mod, gate_idx, seq, tm, tn, casts=()):
    m, k = a.shape
    n = w.shape[1]
    per_seq = seq // tm
    vmem = tm * k * 2 + 2 * k * tn * 2 + 5 * tm * tn * 4 + VMEM_SLACK
    (out,), cast_outs = _host_call(
        _mm_residual_kernel, (m // tm, n // tn),
        [_single_buffered((tm, k), lambda i, j: (i, 0)),
         pl.BlockSpec((k, tn), lambda i, j: (0, j)),
         pl.BlockSpec((tm, tn), lambda i, j: (i, j)),
         pl.BlockSpec((None, 1, tn), lambda i, j: ((i // per_seq) * N_MOD + gate_idx, 0, j))],
        [pl.BlockSpec((tm, tn), lambda i, j: (i, j))],
        [jax.ShapeDtypeStruct((m, n), F32)],
        (a, w, x, mod), casts, vmem, "proj_residual")
    return out, cast_outs


def _sgu_kernel(z_ref, g_ref, w_ref, bt_ref, o_ref, *, n_chunk):
    width = o_ref.shape[1]
    gw = width // SGU_GROUPS
    ti = lax.broadcasted_iota(jnp.int32, (SGU_CHUNK, SGU_CHUNK), 0)
    si = lax.broadcasted_iota(jnp.int32, (SGU_CHUNK, SGU_CHUNK), 1)
    causal = ti >= si
    w_c = [jnp.where(causal, w_ref[g], 0.0).astype(BF16) for g in range(SGU_GROUPS)]
    for c in range(n_chunk):
        rows = slice(c * SGU_CHUNK, (c + 1) * SGU_CHUNK)
        ge = jax.nn.gelu(z_ref[rows, :])
        u = ge[:, :width]
        v = ge[:, width:]
        v = v * lax.rsqrt(jnp.mean(v * v, axis=-1, keepdims=True) + NORM_EPS) * g_ref[...]
        vb = v.astype(BF16)
        for g in range(SGU_GROUPS):
            cols = slice(g * gw, (g + 1) * gw)
            mixed = jnp.dot(w_c[g], vb[:, cols], preferred_element_type=F32) + bt_ref[:, g:g + 1]
            o_ref[rows, cols] = (u[:, cols] * mixed).astype(o_ref.dtype)


def _sgu(z_ab, norm_g, w_s, b_t, layer):
    t = z_ab.shape[0]
    width = BRANCH_WIDTH
    n_chunk = 4
    tm = n_chunk * SGU_CHUNK
    return pl.pallas_call(
        functools.partial(_sgu_kernel, n_chunk=n_chunk),
        grid=(t // tm,),
        in_specs=[pl.BlockSpec((tm, 2 * width), lambda i: (i, 0)),
                  pl.BlockSpec((None, 1, width), lambda i: (layer, 0, 0)),
                  pl.BlockSpec((None, SGU_GROUPS, SGU_CHUNK, SGU_CHUNK), lambda i: (layer, 0, 0, 0)),
                  pl.BlockSpec((None, SGU_CHUNK, SGU_GROUPS), lambda i: (layer, 0, 0))],
        out_specs=pl.BlockSpec((tm, width), lambda i: (i, 0)),
        out_shape=jax.ShapeDtypeStruct((t, width), BF16),
        compiler_params=_params(1, 32 * MIB),
        name="sgu",
    )(z_ab, norm_g, w_s, b_t)


def _pool_kernel(z_ref, w_ref, sc_ref, o_ref):
    seq = z_ref.shape[0]
    pw = POOL_GROUP_WIDTH
    row = lax.broadcasted_iota(jnp.int32, (seq, pw), 0)
    pos1 = (lax.broadcasted_iota(jnp.int32, (seq, 1), 0) + 1).astype(F32)
    for g, window in enumerate(POOL_WINDOWS):
        cols = slice(g * pw, (g + 1) * pw)
        z = z_ref[:, cols]
        s = z
        shift = 1
        while shift < window:
            s = s + jnp.where(row >= shift, pltpu.roll(s, shift, 0), 0.0)
            shift *= 2
        count = jnp.minimum(pos1, float(window))
        mixed = (s / count - z).astype(BF16)
        y = jnp.dot(mixed, w_ref[g], preferred_element_type=F32) * sc_ref[:, cols]
        o_ref[:, cols] = y.astype(o_ref.dtype)


def _pool(z_ab, w_pool, scale, layer, bsz, seq):
    t = z_ab.shape[0]
    width = BRANCH_WIDTH
    pw = POOL_GROUP_WIDTH
    return pl.pallas_call(
        _pool_kernel,
        grid=(bsz,),
        in_specs=[pl.BlockSpec((seq, width), lambda b: (b, 2)),
                  pl.BlockSpec((None, len(POOL_WINDOWS), pw, pw), lambda b: (layer, 0, 0, 0)),
                  pl.BlockSpec((None, 1, width), lambda b: (layer, 0, 0))],
        out_specs=pl.BlockSpec((seq, width), lambda b: (b, 0)),
        out_shape=jax.ShapeDtypeStruct((t, width), BF16),
        compiler_params=_params(1, 48 * MIB),
        name="pool",
    )(z_ab, w_pool, scale)


def _dilated_kernel(q_ref, k_ref, v_ref, kp_ref, vp_ref, o_ref, lse_ref, kbuf, vbuf, *,
                    slopes, n_seq, q_rows, dilation):
    blk = DIL_BLOCK
    n_blk = q_rows // blk
    chunk = pl.program_id(1)
    first_residue = (pl.program_id(0) * n_seq) % dilation
    qi = lax.broadcasted_iota(jnp.int32, (blk, 2 * blk), 0)
    kk = lax.broadcasted_iota(jnp.int32, (blk, 2 * blk), 1)
    steps = qi - kk + blk
    in_window = (steps >= 0) & (steps <= blk)
    steps_f = steps.astype(F32)
    lane = lax.broadcasted_iota(jnp.int32, (blk, HEAD_DIM), 1)
    for s in range(n_seq):
        kbuf[0:blk, :] = kp_ref[s]
        kbuf[blk:, :] = k_ref[s]
        vbuf[0:blk, :] = vp_ref[s]
        vbuf[blk:, :] = v_ref[s]

        def block_body(j, carry, s=s):
            r0 = pl.multiple_of(j * blk, blk)
            key_sub = (chunk * n_blk + j - 1) * blk + kk
            valid = in_window & (key_sub >= 0)
            lse_tile = jnp.zeros((blk, HEAD_DIM), F32)
            if dilation == 1:
                out_rows = pl.ds(r0, blk)
            else:
                start = (chunk * q_rows + r0) * dilation + first_residue + s
                out_rows = pl.ds(start, blk, stride=dilation)
            for h in range(N_HEADS):
                cols = slice(h * HEAD_DIM, (h + 1) * HEAD_DIM)
                q = q_ref[s, pl.ds(r0, blk), cols]
                k = kbuf[pl.ds(r0, 2 * blk), cols]
                v = vbuf[pl.ds(r0, 2 * blk), cols]
                sc = lax.dot_general(q, k, (((1,), (1,)), ((), ())),
                                     preferred_element_type=F32) * ATTN_SCALE
                sc = jnp.where(valid, sc - slopes[h] * steps_f, NEG_INF)
                m = jnp.max(sc, axis=-1, keepdims=True)
                p = jnp.exp(sc - m)
                den = jnp.sum(p, axis=-1, keepdims=True)
                o = jnp.dot(p.astype(BF16), v, preferred_element_type=F32) / den
                o_ref[h, out_rows, :] = o
                lse_tile = jnp.where(lane == h, m + jnp.log(den), lse_tile)
            lse_ref[out_rows, :] = lse_tile
            return carry

        lax.fori_loop(0, n_blk, block_body, 0)


def _dilated_attention(z_g, dilation, group, bsz, seq):
    width = BRANCH_WIDTH
    sub = seq // dilation
    n_sub_seq = bsz * dilation
    arr = z_g.reshape(n_sub_seq, sub, 3 * width)
    q_rows = min(sub, 512)
    n_seq = min(max(1, 1024 // sub), dilation)
    assert dilation % n_seq == 0 and sub % q_rows == 0
    assert dilation == 1 or q_rows == sub
    blocks_per_chunk = q_rows // DIL_BLOCK
    slopes = tuple(float(s) * dilation for s in _alibi_slopes()[group::ALIBI_SETS])

    def prev_block(c):
        return jnp.maximum(c * blocks_per_chunk - 1, 0)

    if dilation == 1:
        assert n_seq == 1
        out_specs = [pl.BlockSpec((None, N_HEADS, q_rows, HEAD_DIM), lambda s, c: (s, 0, c, 0)),
                     pl.BlockSpec((None, q_rows, LANES), lambda s, c: (s, c, 0))]
    else:
        out_specs = [pl.BlockSpec((None, N_HEADS, seq, HEAD_DIM),
                                  lambda s, c: (s * n_seq // dilation, 0, 0, 0)),
                     pl.BlockSpec((None, seq, LANES), lambda s, c: (s * n_seq // dilation, 0, 0))]
    return pl.pallas_call(
        functools.partial(_dilated_kernel, slopes=slopes, n_seq=n_seq, q_rows=q_rows, dilation=dilation),
        grid=(n_sub_seq // n_seq, sub // q_rows),
        in_specs=[pl.BlockSpec((n_seq, q_rows, width), lambda s, c: (s, c, 0)),
                  pl.BlockSpec((n_seq, q_rows, width), lambda s, c: (s, c, 1)),
                  pl.BlockSpec((n_seq, q_rows, width), lambda s, c: (s, c, 2)),
                  pl.BlockSpec((n_seq, DIL_BLOCK, width), lambda s, c: (s, prev_block(c), 1)),
                  pl.BlockSpec((n_seq, DIL_BLOCK, width), lambda s, c: (s, prev_block(c), 2))],
        out_specs=out_specs,
        out_shape=[jax.ShapeDtypeStruct((bsz, N_HEADS, seq, HEAD_DIM), F32),
                   jax.ShapeDtypeStruct((bsz, seq, LANES), F32)],
        scratch_shapes=[pltpu.VMEM((DIL_BLOCK + q_rows, width), BF16),
                        pltpu.VMEM((DIL_BLOCK + q_rows, width), BF16)],
        compiler_params=_params(2, 48 * MIB),
        name="dilated_attn",
    )(arr, arr, arr, arr, arr)


def _dilated_combine_kernel(o0, o1, o2, l0, l1, l2, y_ref):
    for h in range(N_HEADS):
        cols = slice(h * HEAD_DIM, (h + 1) * HEAD_DIM)
        a0 = l0[:, h:h + 1]
        a1 = l1[:, h:h + 1]
        a2 = l2[:, h:h + 1]
        m = jnp.maximum(jnp.maximum(a0, a1), a2)
        e0 = jnp.exp(a0 - m)
        e1 = jnp.exp(a1 - m)
        e2 = jnp.exp(a2 - m)
        y = (e0 * o0[h] + e1 * o1[h] + e2 * o2[h]) / (e0 + e1 + e2)
        y_ref[:, cols] = y.astype(y_ref.dtype)


def _dilated_combine(outs, lses):
    bsz, _, seq, _ = outs[0].shape
    width = BRANCH_WIDTH
    tm = 512
    per_seq = seq // tm
    o_spec = pl.BlockSpec((None, N_HEADS, tm, HEAD_DIM), lambda i: (i // per_seq, 0, i % per_seq, 0))
    l_spec = pl.BlockSpec((None, tm, LANES), lambda i: (i // per_seq, i % per_seq, 0))
    return pl.pallas_call(
        _dilated_combine_kernel,
        grid=(bsz * per_seq,),
        in_specs=[o_spec, o_spec, o_spec, l_spec, l_spec, l_spec],
        out_specs=pl.BlockSpec((tm, width), lambda i: (i, 0)),
        out_shape=jax.ShapeDtypeStruct((bsz * seq, width), BF16),
        compiler_params=_params(1, 32 * MIB),
        name="dilated_combine",
    )(*outs, *lses)


def _dilated_mixer(z_groups, bsz, seq):
    results = [_dilated_attention(z_groups[g], d, g, bsz, seq) for g, d in enumerate(DIL_DILATIONS)]
    return _dilated_combine([o for o, _ in results], [lse for _, lse in results])


def _moba_kernel(slopes_ref, q_ref, k_ref, v_ref, o_ref):
    blk = MOBA_BLOCK
    seq = k_ref.shape[0]
    n_blk = seq // blk
    slope = slopes_ref[pl.program_id(1)]

    k_mean = jnp.sum(k_ref[...].astype(F32).reshape(n_blk, blk, HEAD_DIM), axis=1) * (1.0 / blk)
    k_mean = jnp.concatenate([k_mean, jnp.zeros((HEAD_DIM - n_blk, HEAD_DIM), F32)], axis=0)
    gate_all = lax.dot_general(q_ref[...].astype(F32), k_mean, (((1,), (1,)), ((), ())),
                               precision=lax.Precision.HIGHEST, preferred_element_type=F32)
    lane = lax.broadcasted_iota(jnp.int32, (blk, HEAD_DIM), 1)
    qi = lax.broadcasted_iota(jnp.int32, (blk, blk), 0)
    ki = lax.broadcasted_iota(jnp.int32, (blk, blk), 1)
    causal = qi >= ki
    bias0 = slope * (qi - ki).astype(F32)

    for i in range(n_blk):
        rows = slice(i * blk, (i + 1) * blk)
        q = q_ref[rows, :]
        if i > MOBA_TOPK:
            gate = jnp.where(lane < i, gate_all[rows, :], NEG_INF)
            rank = jnp.zeros((blk, HEAD_DIM), jnp.int32)
            for m in range(i):
                col = gate[:, m:m + 1]
                ahead = (col > gate) | ((col == gate) & (lane > m))
                rank = rank + ahead.astype(jnp.int32)
        scores = []
        for n in range(i + 1):
            kn = k_ref[n * blk:(n + 1) * blk, :]
            s = lax.dot_general(q, kn, (((1,), (1,)), ((), ())), preferred_element_type=F32) * ATTN_SCALE
            s = s - (bias0 + slope * float((i - n) * blk))
            if n == i:
                s = jnp.where(causal, s, NEG_INF)
            elif i > MOBA_TOPK:
                s = jnp.where(rank[:, n:n + 1] < MOBA_TOPK, s, NEG_INF)
            scores.append(s)
        m_row = functools.reduce(jnp.maximum, [jnp.max(s, axis=-1, keepdims=True) for s in scores])
        den = jnp.zeros((blk, 1), F32)
        acc = jnp.zeros((blk, HEAD_DIM), F32)
        for n, s in enumerate(scores):
            p = jnp.exp(s - m_row)
            den = den + jnp.sum(p, axis=-1, keepdims=True)
            acc = acc + jnp.dot(p.astype(BF16), v_ref[n * blk:(n + 1) * blk, :],
                                preferred_element_type=F32)
        o_ref[rows, :] = (acc / den).astype(o_ref.dtype)


def _moba_mixer(z_d, bsz, seq):
    width = BRANCH_WIDTH
    z3 = z_d.reshape(bsz, seq, 3 * width)
    slopes = jnp.asarray(_alibi_slopes()[len(DIL_DILATIONS)::ALIBI_SETS])
    spec = lambda off: pl.BlockSpec((None, seq, HEAD_DIM), lambda b, h: (b, 0, off + h))
    out = pl.pallas_call(
        _moba_kernel,
        grid=(bsz, N_HEADS),
        in_specs=[pl.BlockSpec(memory_space=pltpu.SMEM), spec(0), spec(N_HEADS), spec(2 * N_HEADS)],
        out_specs=spec(0),
        out_shape=jax.ShapeDtypeStruct((bsz, seq, width), BF16),
        compiler_params=_params(2, 32 * MIB),
        name="moba",
    )(slopes, z3, z3, z3)
    return out.reshape(bsz * seq, width)


def _merge_kernel(h_ref, ya_ref, yb_ref, yc_ref, yd_ref, mw0, mw1, mw2, mw3, mb_ref, bw_ref, o_ref):
    h = h_ref[...]
    acc = None
    for b, (y_ref, mw_ref) in enumerate(zip((ya_ref, yb_ref, yc_ref, yd_ref), (mw0, mw1, mw2, mw3))):
        gate = jnp.dot(h, mw_ref[...], preferred_element_type=F32) + mb_ref[b:b + 1, :]
        proj = jnp.dot(y_ref[...], bw_ref[b], preferred_element_type=F32)
        term = jax.nn.sigmoid(gate) * proj
        acc = term if acc is None else acc + term
    o_ref[...] = acc.astype(o_ref.dtype)


def _merge(h, ys, merge_w, merge_b, branch_w, layer, casts=()):
    t, d = h.shape
    width = BRANCH_WIDTH
    tm, tn = 1024, 256
    n_j = d // tn
    mw_spec = lambda b: pl.BlockSpec((None, d, tn), lambda i, j: (layer, 0, b * n_j + j))
    y_spec = _single_buffered((tm, width), lambda i, j: (i, 0))
    vmem = (tm * d * 2 + 4 * tm * width * 2 + 2 * 4 * d * tn * 2 + 2 * 4 * width * tn * 2
            + 6 * tm * tn * 4 + VMEM_SLACK)
    (out,), cast_outs = _host_call(
        _merge_kernel, (t // tm, n_j),
        [_single_buffered((tm, d), lambda i, j: (i, 0)),
         y_spec, y_spec, y_spec, y_spec,
         mw_spec(0), mw_spec(1), mw_spec(2), mw_spec(3),
         pl.BlockSpec((None, 4, tn), lambda i, j: (layer, 0, j)),
         pl.BlockSpec((4, width, tn), lambda i, j: (0, 0, j))],
        [pl.BlockSpec((tm, tn), lambda i, j: (i, j))],
        [jax.ShapeDtypeStruct((t, d), BF16)],
        (h, *ys, merge_w, merge_w, merge_w, merge_w, merge_b, branch_w), casts, vmem, "merge")
    return out, cast_outs


def _ffn_up_kernel(h_ref, wg_ref, wu_ref, cw_ref, cb_ref, o_ref):
    seq = h_ref.shape[0]
    parts = [slice(r, r + FFN_ROW_SPLIT) for r in range(0, seq, FFN_ROW_SPLIT)]

    def project(w_ref):
        return jnp.concatenate([jnp.dot(h_ref[rows, :], w_ref[...], preferred_element_type=F32)
                                for rows in parts], axis=0)

    g = project(wg_ref)
    row = lax.broadcasted_iota(jnp.int32, g.shape, 0)
    g1 = jnp.where(row >= 1, pltpu.roll(g, 1, 0), 0.0)
    g2 = jnp.where(row >= 2, pltpu.roll(g, 2, 0), 0.0)
    a = cw_ref[2:3, :] * g + cw_ref[1:2, :] * g1 + cw_ref[0:1, :] * g2 + cb_ref[...]
    u = project(wu_ref)
    o_ref[...] = (jax.nn.gelu(a) * u).astype(o_ref.dtype)


def _ffn_up(h2, wg, wu, conv_w, conv_b, layer, bsz, seq, casts=()):
    t, d = h2.shape
    f = wg.shape[1]
    tn = 256
    assert seq % FFN_ROW_SPLIT == 0
    w_spec = pl.BlockSpec((d, tn), lambda b, j: (0, j))
    vmem = seq * d * 2 + 2 * 2 * d * tn * 2 + 8 * seq * tn * 4 + VMEM_SLACK
    (out,), cast_outs = _host_call(
        _ffn_up_kernel, (bsz, f // tn),
        [_single_buffered((seq, d), lambda b, j: (b, 0)),
         w_spec, w_spec,
         pl.BlockSpec((None, 3, tn), lambda b, j: (layer, 0, j)),
         pl.BlockSpec((None, 1, tn), lambda b, j: (layer, 0, j))],
        [pl.BlockSpec((seq, tn), lambda b, j: (b, j))],
        [jax.ShapeDtypeStruct((t, f), BF16)],
        (h2, wg, wu, conv_w, conv_b), casts, vmem, "ffn_up")
    return out, cast_outs


def kernel(x, c, norm1_g, w_ada, b_ada, w_in, sgu_norm_g, sgu_w, sgu_b, pool_w, pool_scale, merge_w,
           merge_b, branch_w, out_w, norm2_g, ffn_wg, ffn_wu, conv_w, conv_b, ffn_wd, final_g):
    bsz, seq, d = x.shape
    depth = w_in.shape[0]
    width = BRANCH_WIDTH
    t = bsz * seq
    assert seq % 512 == 0 and seq // DIL_DILATIONS[-1] >= DIL_BLOCK and d == 4 * width

    w_in_b = w_in[0].astype(BF16)
    merge_w_b = merge_w.reshape(depth, d, 4 * d).astype(BF16)
    pool_w_b = pool_w.astype(BF16)
    branch_w2 = branch_w.reshape(depth, 4 * width, d)
    sgu_b_t = sgu_b.transpose(0, 2, 1)
    c_pad = jnp.zeros((8, d), BF16).at[:bsz].set(c.astype(BF16))
    row3 = lambda a: a.reshape(depth, 1, a.shape[-1])
    n_groups = len(DIL_DILATIONS)

    xf = x.reshape(t, d)
    for layer in range(depth):
        mod = _ada_mod(c_pad, w_ada, row3(b_ada), layer)[:bsz].reshape(bsz * N_MOD, 1, d)

        h, *h_res = _norm_mod(xf, row3(norm1_g), layer, mod, 0, 1, seq, DIL_DILATIONS[1:])
        h_by_group = [h] + [hr.reshape(t, d) for hr in h_res]
        z_ab, (branch_w_b,) = _matmul(h, w_in_b, 0, 3 * width, F32, 1024, 512,
                                      casts=[(branch_w2, layer)])
        z_groups = []
        for g in range(n_groups):
            z_g, cast_outs = _matmul(h_by_group[g], w_in_b, (3 + g) * width, 3 * width, BF16, 1024, 512,
                                     run=width, run_stride=n_groups * width,
                                     casts=[(out_w, layer)] if g == 0 else [])
            z_groups.append(z_g)
            if g == 0:
                (out_w_b,) = cast_outs
        z_d, _ = _matmul(h, w_in_b, (3 + 3 * n_groups) * width, 3 * width, BF16, 1024, 512)
        y_a = _sgu(z_ab, row3(sgu_norm_g), sgu_w, sgu_b_t, layer)
        y_b = _pool(z_ab, pool_w_b, row3(pool_scale), layer, bsz, seq)
        y_c = _dilated_mixer(z_groups, bsz, seq)
        y_d = _moba_mixer(z_d, bsz, seq)
        merged, (ffn_wg_b,) = _merge(h, (y_a, y_b, y_c, y_d), merge_w_b, merge_b,
                                     branch_w_b.reshape(4, width, d), layer, casts=[(ffn_wg, layer)])
        xf, (ffn_wu_b,) = _matmul_residual(merged, out_w_b, xf, mod, 2, seq, 1024, 512,
                                           casts=[(ffn_wu, layer)])

        h2 = _norm_mod(xf, row3(norm2_g), layer, mod, 3, 4, seq)
        next_w_in = [(w_in, layer + 1)] if layer + 1 < depth else []
        f, cast_outs = _ffn_up(h2, ffn_wg_b, ffn_wu_b, conv_w, row3(conv_b), layer, bsz, seq,
                               casts=[(ffn_wd, layer)] + next_w_in)
        ffn_wd_b = cast_outs[0]
        if next_w_in:
            w_in_b = cast_outs[1]
        xf, _ = _matmul_residual(f, ffn_wd_b, xf, mod, 5, seq, 1024, 256)

    return _final_norm(xf, final_g.reshape(1, d)).reshape(bsz, seq, d)
```

```python
import functools

import jax
import jax.numpy as jnp
import numpy as np
from jax import lax
from jax.experimental import pallas as pl
from jax.experimental.pallas import tpu as pltpu

BF16 = jnp.bfloat16
F32 = jnp.float32

LANES = 128
HEAD_DIM = 128
N_HEADS = 8
BRANCH_WIDTH = N_HEADS * HEAD_DIM
SGU_CHUNK = 128
SGU_GROUPS = 8
POOL_WINDOWS = (2, 4, 8, 16)
POOL_GROUP_WIDTH = BRANCH_WIDTH // len(POOL_WINDOWS)
DIL_DILATIONS = (1, 4, 16)
DIL_BLOCK = 128
MOBA_BLOCK = 256
MOBA_TOPK = 3
N_ALIBI_HEADS = 32
ALIBI_SETS = 4
N_MOD = 6
NORM_EPS = 1e-6
NEG_INF = -1e30
ATTN_SCALE = HEAD_DIM ** -0.5
FFN_ROW_SPLIT = 1024

V7X_VMEM_BYTES = 64 * 2 ** 20
VMEM_LIMIT_CAP = 56 * 2 ** 20
MIB = 2 ** 20
VMEM_SLACK = 8 * MIB


def _alibi_slopes():
    n = N_ALIBI_HEADS
    return np.asarray(2.0 ** (-8.0 * np.arange(1, n + 1) / n), dtype=np.float32)


def _params(n_axes, vmem_bytes):
    limit = min(VMEM_LIMIT_CAP, max(32 * MIB, int(vmem_bytes)))
    return pltpu.CompilerParams(dimension_semantics=("arbitrary",) * n_axes,
                                vmem_limit_bytes=limit)


def _single_buffered(block_shape, index_map):
    return pl.BlockSpec(block_shape, index_map, pipeline_mode=pl.Buffered(1))


def _divisors(n, multiple_of):
    return [v for v in range(multiple_of, n + 1, multiple_of) if n % v == 0]


def _cast_job(src, layer, grid):
    _, rows, cols = src.shape
    n_steps = int(np.prod(grid))
    choices = [(rb * cb, -cb, rb, cb) for rb in _divisors(rows, 16) for cb in _divisors(cols, LANES)
               if (rows // rb) * (cols // cb) <= n_steps]
    _, _, rb, cb = min(choices)
    n_cb = cols // cb
    n_blocks = (rows // rb) * n_cb

    def block(*idx):
        step = jnp.minimum(_linear_step(grid, idx), n_blocks - 1)
        return step // n_cb, step % n_cb

    return dict(src=src,
                in_spec=pl.BlockSpec((None, rb, cb), lambda *idx: (layer,) + block(*idx)),
                out_spec=pl.BlockSpec((rb, cb), lambda *idx: block(*idx)),
                out_shape=jax.ShapeDtypeStruct((rows, cols), BF16),
                vmem=2 * rb * cb * (4 + 2))


def _linear_step(grid, idx):
    step = idx[0]
    for extent, i in zip(grid[1:], idx[1:]):
        step = step * extent + i
    return step


def _branch_job(src, layer, branch, grid):
    _, rows, _, cols = src.shape
    n_steps = int(np.prod(grid))
    rb = min(r for r in _divisors(rows, 16) if rows // r <= n_steps)
    n_blocks = rows // rb
    return dict(src=src, layer=layer, branch=branch, rb=rb, n_blocks=n_blocks, grid=grid,
                out_spec=pl.BlockSpec((rb, cols), lambda *idx: (jnp.minimum(_linear_step(grid, idx),
                                                                            n_blocks - 1), 0)),
                out_shape=jax.ShapeDtypeStruct((rows, cols), BF16),
                scratch=[pltpu.VMEM((rb, cols), F32), pltpu.SemaphoreType.DMA(())],
                vmem=rb * cols * (4 + 2 * 2))


def _branch_copy(job, src_hbm, buf, sem):
    step = _linear_step(job["grid"], [pl.program_id(a) for a in range(len(job["grid"]))])
    row0 = jnp.minimum(step, job["n_blocks"] - 1) * job["rb"]
    return pltpu.make_async_copy(src_hbm.at[job["layer"], pl.ds(row0, job["rb"]), job["branch"], :],
                                 buf, sem)


def _hosted(body, n_in, n_out, n_cast, n_scratch, branch_job):
    n_branch = 0 if branch_job is None else 1

    def kernel_body(*refs):
        ins, refs = refs[:n_in], refs[n_in:]
        srcs, refs = refs[:n_cast], refs[n_cast:]
        branch_src, refs = refs[:n_branch], refs[n_branch:]
        outs, refs = refs[:n_out], refs[n_out:]
        dsts, refs = refs[:n_cast], refs[n_cast:]
        branch_dst, refs = refs[:n_branch], refs[n_branch:]
        scratch, branch_scratch = refs[:n_scratch], refs[n_scratch:]
        if n_branch:
            copy = _branch_copy(branch_job, branch_src[0], *branch_scratch)
            copy.start()
        body(*ins, *outs, *scratch)
        for s_ref, d_ref in zip(srcs, dsts):
            d_ref[...] = s_ref[...].astype(d_ref.dtype)
        if n_branch:
            copy.wait()
            branch_dst[0][...] = branch_scratch[0][...].astype(branch_dst[0].dtype)
    return kernel_body


def _host_call(body, grid, in_specs, out_specs, out_shapes, operands, casts, vmem, name,
               scratch_shapes=(), branch=None):
    jobs = [_cast_job(src, layer, grid) for src, layer in casts]
    bjob = None if branch is None else _branch_job(*branch, grid)
    bjobs = [] if bjob is None else [bjob]
    outs = pl.pallas_call(
        _hosted(body, len(in_specs), len(out_specs), len(jobs), len(scratch_shapes), bjob),
        grid=grid,
        in_specs=(list(in_specs) + [j["in_spec"] for j in jobs]
                  + [pl.BlockSpec(memory_space=pl.ANY) for _ in bjobs]),
        out_specs=list(out_specs) + [j["out_spec"] for j in jobs + bjobs],
        out_shape=list(out_shapes) + [j["out_shape"] for j in jobs + bjobs],
        scratch_shapes=list(scratch_shapes) + [s for j in bjobs for s in j["scratch"]],
        compiler_params=_params(len(grid), vmem + sum(j["vmem"] for j in jobs + bjobs)),
        name=name,
    )(*operands, *[j["src"] for j in jobs + bjobs])
    return outs[:len(out_specs)], outs[len(out_specs):]


def _cast_kernel(s_ref, d_ref):
    d_ref[...] = s_ref[...].astype(d_ref.dtype)


def _cast_standalone(src, layer):
    _, rows, cols = src.shape
    rb = max(r for r in _divisors(rows, 16) if r * cols * 4 <= 4 * MIB)
    return pl.pallas_call(
        _cast_kernel,
        grid=(rows // rb,),
        in_specs=[pl.BlockSpec((None, rb, cols), lambda i: (layer, i, 0))],
        out_specs=pl.BlockSpec((rb, cols), lambda i: (i, 0)),
        out_shape=jax.ShapeDtypeStruct((rows, cols), BF16),
        compiler_params=_params(1, 2 * rb * cols * 6 + VMEM_SLACK),
        name="cast_weight",
    )(src)


def _ada_kernel(c_ref, w_ref, b_ref, o_ref):
    w = w_ref[...].astype(BF16)
    o_ref[...] = jnp.dot(c_ref[...], w, preferred_element_type=F32) + b_ref[...]


def _ada_mod(c_pad, w_ada, b_ada, layer):
    d = c_pad.shape[1]
    n = w_ada.shape[2]
    tn = 512
    return pl.pallas_call(
        _ada_kernel,
        grid=(n // tn,),
        in_specs=[pl.BlockSpec((8, d), lambda j: (0, 0)),
                  pl.BlockSpec((None, d, tn), lambda j: (layer, 0, j)),
                  pl.BlockSpec((None, 1, tn), lambda j: (layer, 0, j))],
        out_specs=pl.BlockSpec((8, tn), lambda j: (0, j)),
        out_shape=jax.ShapeDtypeStruct((8, n), F32),
        compiler_params=_params(1, 2 * d * tn * 4 + 3 * d * tn * 2 + 8 * MIB),
        name="ada_mod",
    )(c_pad, w_ada, b_ada)


def _norm_mod_kernel(x_ref, g_ref, sc_ref, sh_ref, o_ref):
    x = x_ref[...]
    y = x * lax.rsqrt(jnp.mean(x * x, axis=-1, keepdims=True) + NORM_EPS) * g_ref[...]
    o_ref[...] = (y * (1.0 + sc_ref[...]) + sh_ref[...]).astype(o_ref.dtype)


def _norm_mod_residue_kernel(x_ref, g_ref, sc_ref, sh_ref, o_ref, *rest):
    perm_refs, hbuf = rest[:-1], rest[-1]
    x = x_ref[...]
    y = x * lax.rsqrt(jnp.mean(x * x, axis=-1, keepdims=True) + NORM_EPS) * g_ref[...]
    h = y * (1.0 + sc_ref[...]) + sh_ref[...]
    o_ref[...] = h.astype(o_ref.dtype)
    for c in range(hbuf.shape[0]):
        cols = slice(c * LANES, (c + 1) * LANES)
        hbuf[c] = h[:, cols]
        for p_ref in perm_refs:
            d, n = p_ref.shape[0], p_ref.shape[1]
            for r in range(d):
                p_ref[r, :, cols] = hbuf[c, pl.ds(r, n, stride=d), :].astype(p_ref.dtype)


def _norm_kernel(x_ref, g_ref, o_ref):
    x = x_ref[...]
    y = x * lax.rsqrt(jnp.mean(x * x, axis=-1, keepdims=True) + NORM_EPS) * g_ref[...]
    o_ref[...] = y.astype(o_ref.dtype)


def _norm_mod(x, g, layer, mod, shift_idx, scale_idx, seq, dilations=()):
    t, d = x.shape
    tm = 256
    per_seq = seq // tm
    bsz = t // seq
    in_specs = [pl.BlockSpec((tm, d), lambda i: (i, 0)),
                pl.BlockSpec((None, 1, d), lambda i: (layer, 0, 0)),
                pl.BlockSpec((None, 1, d), lambda i: ((i // per_seq) * N_MOD + scale_idx, 0, 0)),
                pl.BlockSpec((None, 1, d), lambda i: ((i // per_seq) * N_MOD + shift_idx, 0, 0))]
    main_spec = pl.BlockSpec((tm, d), lambda i: (i, 0))
    main_shape = jax.ShapeDtypeStruct((t, d), BF16)
    if not dilations:
        return pl.pallas_call(
            _norm_mod_kernel, grid=(t // tm,), in_specs=in_specs, out_specs=main_spec,
            out_shape=main_shape, compiler_params=_params(1, 6 * tm * d * 4 + VMEM_SLACK),
            name="norm_mod",
        )(x, g, mod, mod)
    perm_specs = [pl.BlockSpec((None, dd, tm // dd, d), lambda i: (i // per_seq, 0, i % per_seq, 0))
                  for dd in dilations]
    perm_shapes = [jax.ShapeDtypeStruct((bsz, dd, seq // dd, d), BF16) for dd in dilations]
    return pl.pallas_call(
        _norm_mod_residue_kernel, grid=(t // tm,), in_specs=in_specs,
        out_specs=[main_spec] + perm_specs, out_shape=[main_shape] + perm_shapes,
        scratch_shapes=[pltpu.VMEM((d // LANES, tm, LANES), F32)],
        compiler_params=_params(1, (7 + 2 * len(dilations)) * tm * d * 4 + VMEM_SLACK),
        name="norm_mod_residue",
    )(x, g, mod, mod)


def _final_norm(x, g):
    t, d = x.shape
    tm = 256
    return pl.pallas_call(
        _norm_kernel,
        grid=(t // tm,),
        in_specs=[pl.BlockSpec((tm, d), lambda i: (i, 0)),
                  pl.BlockSpec((1, d), lambda i: (0, 0))],
        out_specs=pl.BlockSpec((tm, d), lambda i: (i, 0)),
        out_shape=jax.ShapeDtypeStruct((t, d), F32),
        compiler_params=_params(1, 6 * tm * d * 4 + 8 * MIB),
        name="final_norm",
    )(x, g)


def _mm_kernel(a_ref, w_ref, o_ref):
    o_ref[...] = jnp.dot(a_ref[...], w_ref[...], preferred_element_type=F32).astype(o_ref.dtype)


def _mm_residual_kernel(a_ref, w_ref, x_ref, g_ref, o_ref):
    acc = jnp.dot(a_ref[...], w_ref[...], preferred_element_type=F32)
    o_ref[...] = x_ref[...] + g_ref[...] * acc


def _matmul(a, w, col0, n_cols, out_dtype, tm, tn, run=None, run_stride=None, casts=(), branch=None):
    m, k = a.shape
    run = n_cols if run is None else run
    run_stride = run if run_stride is None else run_stride
    assert col0 % tn == 0 and run % tn == 0 and run_stride % tn == 0 and n_cols % run == 0
    per_run = run // tn
    col_block = lambda j: (col0 + (j // per_run) * run_stride) // tn + j % per_run
    vmem = 2 * tm * k * 2 + 2 * k * tn * 2 + 3 * tm * tn * 4 + VMEM_SLACK
    (out,), cast_outs = _host_call(
        _mm_kernel, (m // tm, n_cols // tn),
        [pl.BlockSpec((tm, k), lambda i, j: (i, 0)),
         pl.BlockSpec((k, tn), lambda i, j: (0, col_block(j)))],
        [pl.BlockSpec((tm, tn), lambda i, j: (i, j))],
        [jax.ShapeDtypeStruct((m, n_cols), out_dtype)],
        (a, w), casts, vmem, "proj", branch=branch)
    return out, cast_outs


def _matmul_residual(a, w, x, mod, gate_idx, seq, tm, tn, casts=()):
    m, k = a.shape
    n = w.shape[1]
    per_seq = seq // tm
    vmem = tm * k * 2 + 2 * k * tn * 2 + 5 * tm * tn * 4 + VMEM_SLACK
    (out,), cast_outs = _host_call(
        _mm_residual_kernel, (m // tm, n // tn),
        [_single_buffered((tm, k), lambda i, j: (i, 0)),
         pl.BlockSpec((k, tn), lambda i, j: (0, j)),
         pl.BlockSpec((tm, tn), lambda i, j: (i, j)),
         pl.BlockSpec((None, 1, tn), lambda i, j: ((i // per_seq) * N_MOD + gate_idx, 0, j))],
        [pl.BlockSpec((tm, tn), lambda i, j: (i, j))],
        [jax.ShapeDtypeStruct((m, n), F32)],
        (a, w, x, mod), casts, vmem, "proj_residual")
    return out, cast_outs


def _sgu_kernel(z_ref, g_ref, w_ref, bt_ref, o_ref, *, n_chunk):
    width = o_ref.shape[1]
    gw = width // SGU_GROUPS
    ti = lax.broadcasted_iota(jnp.int32, (SGU_CHUNK, SGU_CHUNK), 0)
    si = lax.broadcasted_iota(jnp.int32, (SGU_CHUNK, SGU_CHUNK), 1)
    causal = ti >= si
    w_c = [jnp.where(causal, w_ref[g], 0.0).astype(BF16) for g in range(SGU_GROUPS)]
    for c in range(n_chunk):
        rows = slice(c * SGU_CHUNK, (c + 1) * SGU_CHUNK)
        ge = jax.nn.gelu(z_ref[rows, :])
        u = ge[:, :width]
        v = ge[:, width:]
        v = v * lax.rsqrt(jnp.mean(v * v, axis=-1, keepdims=True) + NORM_EPS) * g_ref[...]
        vb = v.astype(BF16)
        for g in range(SGU_GROUPS):
            cols = slice(g * gw, (g + 1) * gw)
            mixed = jnp.dot(w_c[g], vb[:, cols], preferred_element_type=F32) + bt_ref[:, g:g + 1]
            o_ref[rows, cols] = (u[:, cols] * mixed).astype(o_ref.dtype)


def _sgu(z_ab, norm_g, w_s, b_t, layer):
    t = z_ab.shape[0]
    width = BRANCH_WIDTH
    n_chunk = 4
    tm = n_chunk * SGU_CHUNK
    return pl.pallas_call(
        functools.partial(_sgu_kernel, n_chunk=n_chunk),
        grid=(t // tm,),
        in_specs=[pl.BlockSpec((tm, 2 * width), lambda i: (i, 0)),
                  pl.BlockSpec((None, 1, width), lambda i: (layer, 0, 0)),
                  pl.BlockSpec((None, SGU_GROUPS, SGU_CHUNK, SGU_CHUNK), lambda i: (layer, 0, 0, 0)),
                  pl.BlockSpec((None, SGU_CHUNK, SGU_GROUPS), lambda i: (layer, 0, 0))],
        out_specs=pl.BlockSpec((tm, width), lambda i: (i, 0)),
        out_shape=jax.ShapeDtypeStruct((t, width), BF16),
        compiler_params=_params(1, 32 * MIB),
        name="sgu",
    )(z_ab, norm_g, w_s, b_t)


def _pool_kernel(z_ref, w_ref, sc_ref, o_ref):
    seq = z_ref.shape[0]
    pw = POOL_GROUP_WIDTH
    row = lax.broadcasted_iota(jnp.int32, (seq, pw), 0)
    pos1 = (lax.broadcasted_iota(jnp.int32, (seq, 1), 0) + 1).astype(F32)
    for g, window in enumerate(POOL_WINDOWS):
        cols = slice(g * pw, (g + 1) * pw)
        z = z_ref[:, cols]
        s = z
        shift = 1
        while shift < window:
            s = s + jnp.where(row >= shift, pltpu.roll(s, shift, 0), 0.0)
            shift *= 2
        count = jnp.minimum(pos1, float(window))
        mixed = (s / count - z).astype(BF16)
        y = jnp.dot(mixed, w_ref[g], preferred_element_type=F32) * sc_ref[:, cols]
        o_ref[:, cols] = y.astype(o_ref.dtype)


def _pool(z_ab, w_pool, scale, layer, bsz, seq):
    t = z_ab.shape[0]
    width = BRANCH_WIDTH
    pw = POOL_GROUP_WIDTH
    return pl.pallas_call(
        _pool_kernel,
        grid=(bsz,),
        in_specs=[pl.BlockSpec((seq, width), lambda b: (b, 2)),
                  pl.BlockSpec((None, len(POOL_WINDOWS), pw, pw), lambda b: (layer, 0, 0, 0)),
                  pl.BlockSpec((None, 1, width), lambda b: (layer, 0, 0))],
        out_specs=pl.BlockSpec((seq, width), lambda b: (b, 0)),
        out_shape=jax.ShapeDtypeStruct((t, width), BF16),
        compiler_params=_params(1, 48 * MIB),
        name="pool",
    )(z_ab, w_pool, scale)


def _dilated_kernel(q_ref, k_ref, v_ref, kp_ref, vp_ref, o_ref, lse_ref, kbuf, vbuf, *,
                    slopes, n_seq, q_rows, dilation):
    blk = DIL_BLOCK
    n_blk = q_rows // blk
    chunk = pl.program_id(1)
    first_residue = (pl.program_id(0) * n_seq) % dilation
    qi = lax.broadcasted_iota(jnp.int32, (blk, 2 * blk), 0)
    kk = lax.broadcasted_iota(jnp.int32, (blk, 2 * blk), 1)
    steps = qi - kk + blk
    in_window = (steps >= 0) & (steps <= blk)
    steps_f = steps.astype(F32)
    lane = lax.broadcasted_iota(jnp.int32, (blk, HEAD_DIM), 1)
    for s in range(n_seq):
        kbuf[0:blk, :] = kp_ref[s]
        kbuf[blk:, :] = k_ref[s]
        vbuf[0:blk, :] = vp_ref[s]
        vbuf[blk:, :] = v_ref[s]

        def block_body(j, carry, s=s):
            r0 = pl.multiple_of(j * blk, blk)
            key_sub = (chunk * n_blk + j - 1) * blk + kk
            valid = in_window & (key_sub >= 0)
            lse_tile = jnp.zeros((blk, HEAD_DIM), F32)
            if dilation == 1:
                out_rows = pl.ds(r0, blk)
            else:
                start = (chunk * q_rows + r0) * dilation + first_residue + s
                out_rows = pl.ds(start, blk, stride=dilation)
            for h in range(N_HEADS):
                cols = slice(h * HEAD_DIM, (h + 1) * HEAD_DIM)
                q = q_ref[s, pl.ds(r0, blk), cols]
                k = kbuf[pl.ds(r0, 2 * blk), cols]
                v = vbuf[pl.ds(r0, 2 * blk), cols]
                sc = lax.dot_general(q, k, (((1,), (1,)), ((), ())),
                                     preferred_element_type=F32) * ATTN_SCALE
                sc = jnp.where(valid, sc - slopes[h] * steps_f, NEG_INF)
                m = jnp.max(sc, axis=-1, keepdims=True)
                p = jnp.exp(sc - m)
                den = jnp.sum(p, axis=-1, keepdims=True)
                o = jnp.dot(p.astype(BF16), v, preferred_element_type=F32) / den
                o_ref[h, out_rows, :] = o
                lse_tile = jnp.where(lane == h, m + jnp.log(den), lse_tile)
            lse_ref[out_rows, :] = lse_tile
            return carry

        lax.fori_loop(0, n_blk, block_body, 0)


def _dilated_attention(z_g, dilation, group, bsz, seq):
    width = BRANCH_WIDTH
    sub = seq // dilation
    n_sub_seq = bsz * dilation
    arr = z_g.reshape(n_sub_seq, sub, 3 * width)
    q_rows = min(sub, 512)
    n_seq = min(max(1, 1024 // sub), dilation)
    assert dilation % n_seq == 0 and sub % q_rows == 0
    assert dilation == 1 or q_rows == sub
    blocks_per_chunk = q_rows // DIL_BLOCK
    slopes = tuple(float(s) * dilation for s in _alibi_slopes()[group::ALIBI_SETS])

    def prev_block(c):
        return jnp.maximum(c * blocks_per_chunk - 1, 0)

    if dilation == 1:
        assert n_seq == 1
        out_specs = [pl.BlockSpec((None, N_HEADS, q_rows, HEAD_DIM), lambda s, c: (s, 0, c, 0)),
                     pl.BlockSpec((None, q_rows, LANES), lambda s, c: (s, c, 0))]
    else:
        out_specs = [pl.BlockSpec((None, N_HEADS, seq, HEAD_DIM),
                                  lambda s, c: (s * n_seq // dilation, 0, 0, 0)),
                     pl.BlockSpec((None, seq, LANES), lambda s, c: (s * n_seq // dilation, 0, 0))]
    return pl.pallas_call(
        functools.partial(_dilated_kernel, slopes=slopes, n_seq=n_seq, q_rows=q_rows, dilation=dilation),
        grid=(n_sub_seq // n_seq, sub // q_rows),
        in_specs=[pl.BlockSpec((n_seq, q_rows, width), lambda s, c: (s, c, 0)),
                  pl.BlockSpec((n_seq, q_rows, width), lambda s, c: (s, c, 1)),
                  pl.BlockSpec((n_seq, q_rows, width), lambda s, c: (s, c, 2)),
                  pl.BlockSpec((n_seq, DIL_BLOCK, width), lambda s, c: (s, prev_block(c), 1)),
                  pl.BlockSpec((n_seq, DIL_BLOCK, width), lambda s, c: (s, prev_block(c), 2))],
        out_specs=out_specs,
        out_shape=[jax.ShapeDtypeStruct((bsz, N_HEADS, seq, HEAD_DIM), F32),
                   jax.ShapeDtypeStruct((bsz, seq, LANES), F32)],
        scratch_shapes=[pltpu.VMEM((DIL_BLOCK + q_rows, width), BF16),
                        pltpu.VMEM((DIL_BLOCK + q_rows, width), BF16)],
        compiler_params=_params(2, 48 * MIB),
        name="dilated_attn",
    )(arr, arr, arr, arr, arr)


def _dilated_combine_kernel(o0, o1, o2, l0, l1, l2, y_ref):
    for h in range(N_HEADS):
        cols = slice(h * HEAD_DIM, (h + 1) * HEAD_DIM)
        a0 = l0[:, h:h + 1]
        a1 = l1[:, h:h + 1]
        a2 = l2[:, h:h + 1]
        m = jnp.maximum(jnp.maximum(a0, a1), a2)
        e0 = jnp.exp(a0 - m)
        e1 = jnp.exp(a1 - m)
        e2 = jnp.exp(a2 - m)
        y = (e0 * o0[h] + e1 * o1[h] + e2 * o2[h]) / (e0 + e1 + e2)
        y_ref[:, cols] = y.astype(y_ref.dtype)


def _dilated_combine(outs, lses):
    bsz, _, seq, _ = outs[0].shape
    width = BRANCH_WIDTH
    tm = 512
    per_seq = seq // tm
    o_spec = pl.BlockSpec((None, N_HEADS, tm, HEAD_DIM), lambda i: (i // per_seq, 0, i % per_seq, 0))
    l_spec = pl.BlockSpec((None, tm, LANES), lambda i: (i // per_seq, i % per_seq, 0))
    return pl.pallas_call(
        _dilated_combine_kernel,
        grid=(bsz * per_seq,),
        in_specs=[o_spec, o_spec, o_spec, l_spec, l_spec, l_spec],
        out_specs=pl.BlockSpec((tm, width), lambda i: (i, 0)),
        out_shape=jax.ShapeDtypeStruct((bsz * seq, width), BF16),
        compiler_params=_params(1, 32 * MIB),
        name="dilated_combine",
    )(*outs, *lses)


def _dilated_mixer(z_groups, bsz, seq):
    results = [_dilated_attention(z_groups[g], d, g, bsz, seq) for g, d in enumerate(DIL_DILATIONS)]
    return _dilated_combine([o for o, _ in results], [lse for _, lse in results])


def _moba_kernel(slopes_ref, q_ref, k_ref, v_ref, o_ref):
    blk = MOBA_BLOCK
    seq = k_ref.shape[0]
    n_blk = seq // blk
    slope = slopes_ref[pl.program_id(1)]

    k_mean = jnp.sum(k_ref[...].astype(F32).reshape(n_blk, blk, HEAD_DIM), axis=1) * (1.0 / blk)
    k_mean = jnp.concatenate([k_mean, jnp.zeros((HEAD_DIM - n_blk, HEAD_DIM), F32)], axis=0)
    gate_all = lax.dot_general(q_ref[...].astype(F32), k_mean, (((1,), (1,)), ((), ())),
                               precision=lax.Precision.HIGHEST, preferred_element_type=F32)
    lane = lax.broadcasted_iota(jnp.int32, (blk, HEAD_DIM), 1)
    qi = lax.broadcasted_iota(jnp.int32, (blk, blk), 0)
    ki = lax.broadcasted_iota(jnp.int32, (blk, blk), 1)
    causal = qi >= ki
    bias0 = slope * (qi - ki).astype(F32)

    for i in range(n_blk):
        rows = slice(i * blk, (i + 1) * blk)
        q = q_ref[rows, :]
        if i > MOBA_TOPK:
            gate = jnp.where(lane < i, gate_all[rows, :], NEG_INF)
            rank = jnp.zeros((blk, HEAD_DIM), jnp.int32)
            for m in range(i):
                col = gate[:, m:m + 1]
                ahead = (col > gate) | ((col == gate) & (lane > m))
                rank = rank + ahead.astype(jnp.int32)
        scores = []
        for n in range(i + 1):
            kn = k_ref[n * blk:(n + 1) * blk, :]
            s = lax.dot_general(q, kn, (((1,), (1,)), ((), ())), preferred_element_type=F32) * ATTN_SCALE
            s = s - (bias0 + slope * float((i - n) * blk))
            if n == i:
                s = jnp.where(causal, s, NEG_INF)
            elif i > MOBA_TOPK:
                s = jnp.where(rank[:, n:n + 1] < MOBA_TOPK, s, NEG_INF)
            scores.append(s)
        m_row = functools.reduce(jnp.maximum, [jnp.max(s, axis=-1, keepdims=True) for s in scores])
        den = jnp.zeros((blk, 1), F32)
        acc = jnp.zeros((blk, HEAD_DIM), F32)
        for n, s in enumerate(scores):
            p = jnp.exp(s - m_row)
            den = den + jnp.sum(p, axis=-1, keepdims=True)
            acc = acc + jnp.dot(p.astype(BF16), v_ref[n * blk:(n + 1) * blk, :],
                                preferred_element_type=F32)
        o_ref[rows, :] = (acc / den).astype(o_ref.dtype)


def _moba_mixer(z_d, bsz, seq):
    width = BRANCH_WIDTH
    z3 = z_d.reshape(bsz, seq, 3 * width)
    slopes = jnp.asarray(_alibi_slopes()[len(DIL_DILATIONS)::ALIBI_SETS])
    spec = lambda off: pl.BlockSpec((None, seq, HEAD_DIM), lambda b, h: (b, 0, off + h))
    out = pl.pallas_call(
        _moba_kernel,
        grid=(bsz, N_HEADS),
        in_specs=[pl.BlockSpec(memory_space=pltpu.SMEM), spec(0), spec(N_HEADS), spec(2 * N_HEADS)],
        out_specs=spec(0),
        out_shape=jax.ShapeDtypeStruct((bsz, seq, width), BF16),
        compiler_params=_params(2, 32 * MIB),
        name="moba",
    )(slopes, z3, z3, z3)
    return out.reshape(bsz * seq, width)


def _merge_kernel(h_ref, ya_ref, yb_ref, yc_ref, yd_ref, mw0, mw1, mw2, mw3, mb_ref, bw_ref, o_ref):
    h = h_ref[...]
    acc = None
    for b, (y_ref, mw_ref) in enumerate(zip((ya_ref, yb_ref, yc_ref, yd_ref), (mw0, mw1, mw2, mw3))):
        gate = jnp.dot(h, mw_ref[...], preferred_element_type=F32) + mb_ref[b:b + 1, :]
        proj = jnp.dot(y_ref[...], bw_ref[b], preferred_element_type=F32)
        term = jax.nn.sigmoid(gate) * proj
        acc = term if acc is None else acc + term
    o_ref[...] = acc.astype(o_ref.dtype)


def _merge(h, ys, merge_ws, merge_b, branch_w, layer, casts=()):
    t, d = h.shape
    width = BRANCH_WIDTH
    tm, tn = 1024, 256
    n_j = d // tn
    mw_spec = pl.BlockSpec((d, tn), lambda i, j: (0, j))
    y_spec = _single_buffered((tm, width), lambda i, j: (i, 0))
    vmem = (tm * d * 2 + 4 * tm * width * 2 + 2 * 4 * d * tn * 2 + 2 * 4 * width * tn * 2
            + 6 * tm * tn * 4 + VMEM_SLACK)
    (out,), cast_outs = _host_call(
        _merge_kernel, (t // tm, n_j),
        [_single_buffered((tm, d), lambda i, j: (i, 0)),
         y_spec, y_spec, y_spec, y_spec,
         mw_spec, mw_spec, mw_spec, mw_spec,
         pl.BlockSpec((None, 4, tn), lambda i, j: (layer, 0, j)),
         pl.BlockSpec((4, width, tn), lambda i, j: (0, 0, j))],
        [pl.BlockSpec((tm, tn), lambda i, j: (i, j))],
        [jax.ShapeDtypeStruct((t, d), BF16)],
        (h, *ys, *merge_ws, merge_b, branch_w), casts, vmem, "merge")
    return out, cast_outs


def _ffn_up_kernel(h_ref, wg_ref, wu_ref, cw_ref, cb_ref, o_ref):
    seq = h_ref.shape[0]
    parts = [slice(r, r + FFN_ROW_SPLIT) for r in range(0, seq, FFN_ROW_SPLIT)]

    def project(w_ref):
        return jnp.concatenate([jnp.dot(h_ref[rows, :], w_ref[...], preferred_element_type=F32)
                                for rows in parts], axis=0)

    g = project(wg_ref)
    row = lax.broadcasted_iota(jnp.int32, g.shape, 0)
    g1 = jnp.where(row >= 1, pltpu.roll(g, 1, 0), 0.0)
    g2 = jnp.where(row >= 2, pltpu.roll(g, 2, 0), 0.0)
    a = cw_ref[2:3, :] * g + cw_ref[1:2, :] * g1 + cw_ref[0:1, :] * g2 + cb_ref[...]
    u = project(wu_ref)
    o_ref[...] = (jax.nn.gelu(a) * u).astype(o_ref.dtype)


def _ffn_up(h2, wg, wu, conv_w, conv_b, layer, bsz, seq, casts=()):
    t, d = h2.shape
    f = wg.shape[1]
    tn = 256
    assert seq % FFN_ROW_SPLIT == 0
    w_spec = pl.BlockSpec((d, tn), lambda b, j: (0, j))
    vmem = seq * d * 2 + 2 * 2 * d * tn * 2 + 8 * seq * tn * 4 + VMEM_SLACK
    (out,), cast_outs = _host_call(
        _ffn_up_kernel, (bsz, f // tn),
        [_single_buffered((seq, d), lambda b, j: (b, 0)),
         w_spec, w_spec,
         pl.BlockSpec((None, 3, tn), lambda b, j: (layer, 0, j)),
         pl.BlockSpec((None, 1, tn), lambda b, j: (layer, 0, j))],
        [pl.BlockSpec((seq, tn), lambda b, j: (b, j))],
        [jax.ShapeDtypeStruct((t, f), BF16)],
        (h2, wg, wu, conv_w, conv_b), casts, vmem, "ffn_up")
    return out, cast_outs


def kernel(x, c, norm1_g, w_ada, b_ada, w_in, sgu_norm_g, sgu_w, sgu_b, pool_w, pool_scale, merge_w,
           merge_b, branch_w, out_w, norm2_g, ffn_wg, ffn_wu, conv_w, conv_b, ffn_wd, final_g):
    bsz, seq, d = x.shape
    depth = w_in.shape[0]
    width = BRANCH_WIDTH
    t = bsz * seq
    assert seq % 512 == 0 and seq // DIL_DILATIONS[-1] >= DIL_BLOCK and d == 4 * width

    w_in_b = _cast_standalone(w_in, 0)
    pool_w_b = pool_w.astype(BF16)
    branch_w2 = branch_w.reshape(depth, 4 * width, d)
    sgu_b_t = sgu_b.transpose(0, 2, 1)
    c_pad = jnp.zeros((8, d), BF16).at[:bsz].set(c.astype(BF16))
    row3 = lambda a: a.reshape(depth, 1, a.shape[-1])
    n_groups = len(DIL_DILATIONS)

    xf = x.reshape(t, d)
    for layer in range(depth):
        mod = _ada_mod(c_pad, w_ada, row3(b_ada), layer)[:bsz].reshape(bsz * N_MOD, 1, d)

        h, *h_res = _norm_mod(xf, row3(norm1_g), layer, mod, 0, 1, seq, DIL_DILATIONS[1:])
        h_by_group = [h] + [hr.reshape(t, d) for hr in h_res]
        z_ab, (branch_w_b,) = _matmul(h, w_in_b, 0, 3 * width, F32, 1024, 512,
                                      casts=[(branch_w2, layer)])
        z_groups, merge_ws = [], []
        for g in range(n_groups):
            z_g, (mw_b,) = _matmul(h_by_group[g], w_in_b, (3 + g) * width, 3 * width, BF16, 1024, 512,
                                   run=width, run_stride=n_groups * width, branch=(merge_w, layer, g))
            z_groups.append(z_g)
            merge_ws.append(mw_b)
        z_d, (mw_b,) = _matmul(h, w_in_b, (3 + 3 * n_groups) * width, 3 * width, BF16, 1024, 512,
                               branch=(merge_w, layer, n_groups))
        merge_ws.append(mw_b)
        y_a = _sgu(z_ab, row3(sgu_norm_g), sgu_w, sgu_b_t, layer)
        y_b = _pool(z_ab, pool_w_b, row3(pool_scale), layer, bsz, seq)
        y_c = _dilated_mixer(z_groups, bsz, seq)
        y_d = _moba_mixer(z_d, bsz, seq)
        merged, (ffn_wg_b, out_w_b) = _merge(h, (y_a, y_b, y_c, y_d), merge_ws, merge_b,
                                             branch_w_b.reshape(4, width, d), layer,
                                             casts=[(ffn_wg, layer), (out_w, layer)])
        xf, (ffn_wu_b,) = _matmul_residual(merged, out_w_b, xf, mod, 2, seq, 1024, 512,
                                           casts=[(ffn_wu, layer)])

        h2 = _norm_mod(xf, row3(norm2_g), layer, mod, 3, 4, seq)
        next_w_in = [(w_in, layer + 1)] if layer + 1 < depth else []
        f, cast_outs = _ffn_up(h2, ffn_wg_b, ffn_wu_b, conv_w, row3(conv_b), layer, bsz, seq,
                               casts=[(ffn_wd, layer)] + next_w_in)
        ffn_wd_b = cast_outs[0]
        if next_w_in:
            w_in_b = cast_outs[1]
        xf, _ = _matmul_residual(f, ffn_wd_b, xf, mod, 5, seq, 1024, 256)

    return _final_norm(xf, final_g.reshape(1, d)).reshape(bsz, seq, d)
```

```python
import functools

import jax
import jax.numpy as jnp
import numpy as np
from jax import lax
from jax.experimental import pallas as pl
from jax.experimental.pallas import tpu as pltpu

BF16 = jnp.bfloat16
F32 = jnp.float32

LANES = 128
HEAD_DIM = 128
N_HEADS = 8
BRANCH_WIDTH = N_HEADS * HEAD_DIM
SGU_CHUNK = 128
SGU_GROUPS = 8
POOL_WINDOWS = (2, 4, 8, 16)
POOL_GROUP_WIDTH = BRANCH_WIDTH // len(POOL_WINDOWS)
DIL_DILATIONS = (1, 4, 16)
DIL_BLOCK = 128
MOBA_BLOCK = 256
MOBA_TOPK = 3
N_ALIBI_HEADS = 32
ALIBI_SETS = 4
N_MOD = 6
NORM_EPS = 1e-6
NEG_INF = -1e30
ATTN_SCALE = HEAD_DIM ** -0.5
FFN_ROW_SPLIT = 1024

V7X_VMEM_BYTES = 64 * 2 ** 20
VMEM_LIMIT_CAP = 58 * 2 ** 20
MIB = 2 ** 20
VMEM_SLACK = 8 * MIB


def _alibi_slopes():
    n = N_ALIBI_HEADS
    return np.asarray(2.0 ** (-8.0 * np.arange(1, n + 1) / n), dtype=np.float32)


def _params(n_axes, vmem_bytes):
    limit = min(VMEM_LIMIT_CAP, max(32 * MIB, int(vmem_bytes)))
    return pltpu.CompilerParams(dimension_semantics=("arbitrary",) * n_axes,
                                vmem_limit_bytes=limit)


def _single_buffered(block_shape, index_map):
    return pl.BlockSpec(block_shape, index_map, pipeline_mode=pl.Buffered(1))


def _divisors(n, multiple_of):
    return [v for v in range(multiple_of, n + 1, multiple_of) if n % v == 0]


def _column_runs(src_cols, runs):
    return (0, 1, src_cols, src_cols) if runs is None else runs


def _cast_job(src, layer, grid, runs=None):
    _, rows, src_cols = src.shape
    col0, n_runs, run, run_stride = _column_runs(src_cols, runs)
    cols = n_runs * run
    n_steps = int(np.prod(grid))
    choices = [(rb * cb, -cb, rb, cb) for rb in _divisors(rows, 16) for cb in _divisors(run, LANES)
               if (rows // rb) * (cols // cb) <= n_steps and col0 % cb == 0 and run_stride % cb == 0]
    _, _, rb, cb = min(choices)
    n_cb = cols // cb
    per_run = run // cb
    n_blocks = (rows // rb) * n_cb

    def block(*idx):
        step = jnp.minimum(_linear_step(grid, idx), n_blocks - 1)
        return step // n_cb, step % n_cb

    def src_block(*idx):
        r, c = block(*idx)
        return layer, r, (col0 + (c // per_run) * run_stride) // cb + c % per_run

    return dict(src=src,
                in_spec=pl.BlockSpec((None, rb, cb), src_block),
                out_spec=pl.BlockSpec((rb, cb), block),
                out_shape=jax.ShapeDtypeStruct((rows, cols), BF16),
                vmem=2 * rb * cb * (4 + 2))


def _linear_step(grid, idx):
    step = idx[0]
    for extent, i in zip(grid[1:], idx[1:]):
        step = step * extent + i
    return step


def _branch_job(src, layer, branch, grid):
    _, rows, _, cols = src.shape
    n_steps = int(np.prod(grid))
    rb = min(r for r in _divisors(rows, 16) if rows // r <= n_steps)
    n_blocks = rows // rb
    return dict(src=src, layer=layer, branch=branch, rb=rb, n_blocks=n_blocks, grid=grid,
                out_spec=pl.BlockSpec((rb, cols), lambda *idx: (jnp.minimum(_linear_step(grid, idx),
                                                                            n_blocks - 1), 0)),
                out_shape=jax.ShapeDtypeStruct((rows, cols), BF16),
                scratch=[pltpu.VMEM((rb, cols), F32), pltpu.SemaphoreType.DMA(())],
                vmem=rb * cols * (4 + 2 * 2))


def _branch_copy(job, src_hbm, buf, sem):
    step = _linear_step(job["grid"], [pl.program_id(a) for a in range(len(job["grid"]))])
    row0 = jnp.minimum(step, job["n_blocks"] - 1) * job["rb"]
    return pltpu.make_async_copy(src_hbm.at[job["layer"], pl.ds(row0, job["rb"]), job["branch"], :],
                                 buf, sem)


def _hosted(body, n_in, n_out, n_cast, n_scratch, branch_job):
    n_branch = 0 if branch_job is None else 1

    def kernel_body(*refs):
        ins, refs = refs[:n_in], refs[n_in:]
        srcs, refs = refs[:n_cast], refs[n_cast:]
        branch_src, refs = refs[:n_branch], refs[n_branch:]
        outs, refs = refs[:n_out], refs[n_out:]
        dsts, refs = refs[:n_cast], refs[n_cast:]
        branch_dst, refs = refs[:n_branch], refs[n_branch:]
        scratch, branch_scratch = refs[:n_scratch], refs[n_scratch:]
        if n_branch:
            copy = _branch_copy(branch_job, branch_src[0], *branch_scratch)
            copy.start()
        body(*ins, *outs, *scratch)
        for s_ref, d_ref in zip(srcs, dsts):
            d_ref[...] = s_ref[...].astype(d_ref.dtype)
        if n_branch:
            copy.wait()
            branch_dst[0][...] = branch_scratch[0][...].astype(branch_dst[0].dtype)
    return kernel_body


def _host_call(body, grid, in_specs, out_specs, out_shapes, operands, casts, vmem, name,
               scratch_shapes=(), branch=None):
    jobs = [_cast_job(c[0], c[1], grid, *c[2:]) for c in casts]
    bjob = None if branch is None else _branch_job(*branch, grid)
    bjobs = [] if bjob is None else [bjob]
    outs = pl.pallas_call(
        _hosted(body, len(in_specs), len(out_specs), len(jobs), len(scratch_shapes), bjob),
        grid=grid,
        in_specs=(list(in_specs) + [j["in_spec"] for j in jobs]
                  + [pl.BlockSpec(memory_space=pl.ANY) for _ in bjobs]),
        out_specs=list(out_specs) + [j["out_spec"] for j in jobs + bjobs],
        out_shape=list(out_shapes) + [j["out_shape"] for j in jobs + bjobs],
        scratch_shapes=list(scratch_shapes) + [s for j in bjobs for s in j["scratch"]],
        compiler_params=_params(len(grid), vmem + sum(j["vmem"] for j in jobs + bjobs)),
        name=name,
    )(*operands, *[j["src"] for j in jobs + bjobs])
    return outs[:len(out_specs)], outs[len(out_specs):]


def _cast_kernel(s_ref, d_ref):
    d_ref[...] = s_ref[...].astype(d_ref.dtype)


def _cast_standalone(src, layer, runs=None):
    _, rows, src_cols = src.shape
    _, n_runs, run, _ = _column_runs(src_cols, runs)
    grid = (max(n_runs, rows * n_runs * run * 4 // (4 * MIB)),)
    job = _cast_job(src, layer, grid, runs)
    return pl.pallas_call(
        _cast_kernel,
        grid=grid,
        in_specs=[job["in_spec"]],
        out_specs=job["out_spec"],
        out_shape=job["out_shape"],
        compiler_params=_params(1, job["vmem"] + VMEM_SLACK),
        name="cast_weight",
    )(src)


def _ada_kernel(c_ref, w_ref, b_ref, o_ref):
    w = w_ref[...].astype(BF16)
    o_ref[...] = jnp.dot(c_ref[...], w, preferred_element_type=F32) + b_ref[...]


def _ada_mod(c_pad, w_ada, b_ada, layer):
    d = c_pad.shape[1]
    n = w_ada.shape[2]
    tn = 512
    return pl.pallas_call(
        _ada_kernel,
        grid=(n // tn,),
        in_specs=[pl.BlockSpec((8, d), lambda j: (0, 0)),
                  pl.BlockSpec((None, d, tn), lambda j: (layer, 0, j)),
                  pl.BlockSpec((None, 1, tn), lambda j: (layer, 0, j))],
        out_specs=pl.BlockSpec((8, tn), lambda j: (0, j)),
        out_shape=jax.ShapeDtypeStruct((8, n), F32),
        compiler_params=_params(1, 2 * d * tn * 4 + 3 * d * tn * 2 + 8 * MIB),
        name="ada_mod",
    )(c_pad, w_ada, b_ada)


def _norm_mod_kernel(x_ref, g_ref, sc_ref, sh_ref, o_ref):
    x = x_ref[...]
    y = x * lax.rsqrt(jnp.mean(x * x, axis=-1, keepdims=True) + NORM_EPS) * g_ref[...]
    o_ref[...] = (y * (1.0 + sc_ref[...]) + sh_ref[...]).astype(o_ref.dtype)


def _norm_mod_residue_kernel(x_ref, g_ref, sc_ref, sh_ref, o_ref, *rest):
    perm_refs, hbuf = rest[:-1], rest[-1]
    x = x_ref[...]
    y = x * lax.rsqrt(jnp.mean(x * x, axis=-1, keepdims=True) + NORM_EPS) * g_ref[...]
    h = y * (1.0 + sc_ref[...]) + sh_ref[...]
    o_ref[...] = h.astype(o_ref.dtype)
    for c in range(hbuf.shape[0]):
        cols = slice(c * LANES, (c + 1) * LANES)
        hbuf[c] = h[:, cols]
        for p_ref in perm_refs:
            d, n = p_ref.shape[0], p_ref.shape[1]
            for r in range(d):
                p_ref[r, :, cols] = hbuf[c, pl.ds(r, n, stride=d), :].astype(p_ref.dtype)


def _norm_kernel(x_ref, g_ref, o_ref):
    x = x_ref[...]
    y = x * lax.rsqrt(jnp.mean(x * x, axis=-1, keepdims=True) + NORM_EPS) * g_ref[...]
    o_ref[...] = y.astype(o_ref.dtype)


def _norm_mod(x, g, layer, mod, shift_idx, scale_idx, seq, dilations=()):
    t, d = x.shape
    tm = 256
    per_seq = seq // tm
    bsz = t // seq
    in_specs = [pl.BlockSpec((tm, d), lambda i: (i, 0)),
                pl.BlockSpec((None, 1, d), lambda i: (layer, 0, 0)),
                pl.BlockSpec((None, 1, d), lambda i: ((i // per_seq) * N_MOD + scale_idx, 0, 0)),
                pl.BlockSpec((None, 1, d), lambda i: ((i // per_seq) * N_MOD + shift_idx, 0, 0))]
    main_spec = pl.BlockSpec((tm, d), lambda i: (i, 0))
    main_shape = jax.ShapeDtypeStruct((t, d), BF16)
    if not dilations:
        return pl.pallas_call(
            _norm_mod_kernel, grid=(t // tm,), in_specs=in_specs, out_specs=main_spec,
            out_shape=main_shape, compiler_params=_params(1, 6 * tm * d * 4 + VMEM_SLACK),
            name="norm_mod",
        )(x, g, mod, mod)
    perm_specs = [pl.BlockSpec((None, dd, tm // dd, d), lambda i: (i // per_seq, 0, i % per_seq, 0))
                  for dd in dilations]
    perm_shapes = [jax.ShapeDtypeStruct((bsz, dd, seq // dd, d), BF16) for dd in dilations]
    return pl.pallas_call(
        _norm_mod_residue_kernel, grid=(t // tm,), in_specs=in_specs,
        out_specs=[main_spec] + perm_specs, out_shape=[main_shape] + perm_shapes,
        scratch_shapes=[pltpu.VMEM((d // LANES, tm, LANES), F32)],
        compiler_params=_params(1, (7 + 2 * len(dilations)) * tm * d * 4 + VMEM_SLACK),
        name="norm_mod_residue",
    )(x, g, mod, mod)


def _final_norm(x, g):
    t, d = x.shape
    tm = 256
    return pl.pallas_call(
        _norm_kernel,
        grid=(t // tm,),
        in_specs=[pl.BlockSpec((tm, d), lambda i: (i, 0)),
                  pl.BlockSpec((1, d), lambda i: (0, 0))],
        out_specs=pl.BlockSpec((tm, d), lambda i: (i, 0)),
        out_shape=jax.ShapeDtypeStruct((t, d), F32),
        compiler_params=_params(1, 6 * tm * d * 4 + 8 * MIB),
        name="final_norm",
    )(x, g)


def _mm_kernel(a_ref, w_ref, o_ref):
    o_ref[...] = jnp.dot(a_ref[...], w_ref[...], preferred_element_type=F32).astype(o_ref.dtype)


def _mm_residual_kernel(a_ref, w_ref, x_ref, g_ref, o_ref):
    acc = jnp.dot(a_ref[...], w_ref[...], preferred_element_type=F32)
    o_ref[...] = x_ref[...] + g_ref[...] * acc


def _matmul(a, w, out_dtype, tm, tn, casts=(), branch=None):
    m, k = a.shape
    n = w.shape[1]
    vmem = 2 * tm * k * 2 + 2 * k * tn * 2 + 3 * tm * tn * 4 + VMEM_SLACK
    (out,), cast_outs = _host_call(
        _mm_kernel, (m // tm, n // tn),
        [pl.BlockSpec((tm, k), lambda i, j: (i, 0)),
         pl.BlockSpec((k, tn), lambda i, j: (0, j))],
        [pl.BlockSpec((tm, tn), lambda i, j: (i, j))],
        [jax.ShapeDtypeStruct((m, n), out_dtype)],
        (a, w), casts, vmem, "proj", branch=branch)
    return out, cast_outs


def _matmul_residual(a, w, x, mod, gate_idx, seq, tm, tn, casts=(), a_buffers=1):
    m, k = a.shape
    n = w.shape[1]
    per_seq = seq // tm
    vmem = a_buffers * tm * k * 2 + 2 * k * tn * 2 + 5 * tm * tn * 4 + VMEM_SLACK
    (out,), cast_outs = _host_call(
        _mm_residual_kernel, (m // tm, n // tn),
        [pl.BlockSpec((tm, k), lambda i, j: (i, 0), pipeline_mode=pl.Buffered(a_buffers)),
         pl.BlockSpec((k, tn), lambda i, j: (0, j)),
         pl.BlockSpec((tm, tn), lambda i, j: (i, j)),
         pl.BlockSpec((None, 1, tn), lambda i, j: ((i // per_seq) * N_MOD + gate_idx, 0, j))],
        [pl.BlockSpec((tm, tn), lambda i, j: (i, j))],
        [jax.ShapeDtypeStruct((m, n), F32)],
        (a, w, x, mod), casts, vmem, "proj_residual")
    return out, cast_outs


def _sgu_kernel(z_ref, g_ref, w_ref, bt_ref, o_ref, *, n_chunk):
    width = o_ref.shape[1]
    gw = width // SGU_GROUPS
    ti = lax.broadcasted_iota(jnp.int32, (SGU_CHUNK, SGU_CHUNK), 0)
    si = lax.broadcasted_iota(jnp.int32, (SGU_CHUNK, SGU_CHUNK), 1)
    causal = ti >= si
    w_c = [jnp.where(causal, w_ref[g], 0.0).astype(BF16) for g in range(SGU_GROUPS)]
    for c in range(n_chunk):
        rows = slice(c * SGU_CHUNK, (c + 1) * SGU_CHUNK)
        ge = jax.nn.gelu(z_ref[rows, :])
        u = ge[:, :width]
        v = ge[:, width:]
        v = v * lax.rsqrt(jnp.mean(v * v, axis=-1, keepdims=True) + NORM_EPS) * g_ref[...]
        vb = v.astype(BF16)
        for g in range(SGU_GROUPS):
            cols = slice(g * gw, (g + 1) * gw)
            mixed = jnp.dot(w_c[g], vb[:, cols], preferred_element_type=F32) + bt_ref[:, g:g + 1]
            o_ref[rows, cols] = (u[:, cols] * mixed).astype(o_ref.dtype)


def _sgu(z_ab, norm_g, w_s, b_t, layer):
    t = z_ab.shape[0]
    width = BRANCH_WIDTH
    n_chunk = 4
    tm = n_chunk * SGU_CHUNK
    return pl.pallas_call(
        functools.partial(_sgu_kernel, n_chunk=n_chunk),
        grid=(t // tm,),
        in_specs=[pl.BlockSpec((tm, 2 * width), lambda i: (i, 0)),
                  pl.BlockSpec((None, 1, width), lambda i: (layer, 0, 0)),
                  pl.BlockSpec((None, SGU_GROUPS, SGU_CHUNK, SGU_CHUNK), lambda i: (layer, 0, 0, 0)),
                  pl.BlockSpec((None, SGU_CHUNK, SGU_GROUPS), lambda i: (layer, 0, 0))],
        out_specs=pl.BlockSpec((tm, width), lambda i: (i, 0)),
        out_shape=jax.ShapeDtypeStruct((t, width), BF16),
        compiler_params=_params(1, 32 * MIB),
        name="sgu",
    )(z_ab, norm_g, w_s, b_t)


def _pool_kernel(z_ref, w_ref, sc_ref, o_ref):
    seq = z_ref.shape[0]
    pw = POOL_GROUP_WIDTH
    row = lax.broadcasted_iota(jnp.int32, (seq, pw), 0)
    pos1 = (lax.broadcasted_iota(jnp.int32, (seq, 1), 0) + 1).astype(F32)
    for g, window in enumerate(POOL_WINDOWS):
        cols = slice(g * pw, (g + 1) * pw)
        z = z_ref[:, cols]
        s = z
        shift = 1
        while shift < window:
            s = s + jnp.where(row >= shift, pltpu.roll(s, shift, 0), 0.0)
            shift *= 2
        count = jnp.minimum(pos1, float(window))
        mixed = (s / count - z).astype(BF16)
        y = jnp.dot(mixed, w_ref[g], preferred_element_type=F32) * sc_ref[:, cols]
        o_ref[:, cols] = y.astype(o_ref.dtype)


def _pool(z_ab, w_pool, scale, layer, bsz, seq):
    t = z_ab.shape[0]
    width = BRANCH_WIDTH
    pw = POOL_GROUP_WIDTH
    return pl.pallas_call(
        _pool_kernel,
        grid=(bsz,),
        in_specs=[pl.BlockSpec((seq, width), lambda b: (b, 2)),
                  pl.BlockSpec((None, len(POOL_WINDOWS), pw, pw), lambda b: (layer, 0, 0, 0)),
                  pl.BlockSpec((None, 1, width), lambda b: (layer, 0, 0))],
        out_specs=pl.BlockSpec((seq, width), lambda b: (b, 0)),
        out_shape=jax.ShapeDtypeStruct((t, width), BF16),
        compiler_params=_params(1, 48 * MIB),
        name="pool",
    )(z_ab, w_pool, scale)


def _dilated_kernel(q_ref, k_ref, v_ref, kp_ref, vp_ref, o_ref, lse_ref, kbuf, vbuf, *,
                    slopes, n_seq, q_rows, dilation):
    blk = DIL_BLOCK
    n_blk = q_rows // blk
    chunk = pl.program_id(1)
    first_residue = (pl.program_id(0) * n_seq) % dilation
    qi = lax.broadcasted_iota(jnp.int32, (blk, 2 * blk), 0)
    kk = lax.broadcasted_iota(jnp.int32, (blk, 2 * blk), 1)
    steps = qi - kk + blk
    in_window = (steps >= 0) & (steps <= blk)
    steps_f = steps.astype(F32)
    lane = lax.broadcasted_iota(jnp.int32, (blk, HEAD_DIM), 1)
    for s in range(n_seq):
        kbuf[0:blk, :] = kp_ref[s]
        kbuf[blk:, :] = k_ref[s]
        vbuf[0:blk, :] = vp_ref[s]
        vbuf[blk:, :] = v_ref[s]

        def block_body(j, carry, s=s):
            r0 = pl.multiple_of(j * blk, blk)
            key_sub = (chunk * n_blk + j - 1) * blk + kk
            valid = in_window & (key_sub >= 0)
            lse_tile = jnp.zeros((blk, HEAD_DIM), F32)
            if dilation == 1:
                out_rows = pl.ds(r0, blk)
            else:
                start = (chunk * q_rows + r0) * dilation + first_residue + s
                out_rows = pl.ds(start, blk, stride=dilation)
            for h in range(N_HEADS):
                cols = slice(h * HEAD_DIM, (h + 1) * HEAD_DIM)
                q = q_ref[s, pl.ds(r0, blk), cols]
                k = kbuf[pl.ds(r0, 2 * blk), cols]
                v = vbuf[pl.ds(r0, 2 * blk), cols]
                sc = lax.dot_general(q, k, (((1,), (1,)), ((), ())),
                                     preferred_element_type=F32) * ATTN_SCALE
                sc = jnp.where(valid, sc - slopes[h] * steps_f, NEG_INF)
                m = jnp.max(sc, axis=-1, keepdims=True)
                p = jnp.exp(sc - m)
                den = jnp.sum(p, axis=-1, keepdims=True)
                o = jnp.dot(p.astype(BF16), v, preferred_element_type=F32) / den
                o_ref[h, out_rows, :] = o
                lse_tile = jnp.where(lane == h, m + jnp.log(den), lse_tile)
            lse_ref[out_rows, :] = lse_tile
            return carry

        lax.fori_loop(0, n_blk, block_body, 0)


def _dilated_attention(z_g, dilation, group, bsz, seq):
    width = BRANCH_WIDTH
    sub = seq // dilation
    n_sub_seq = bsz * dilation
    arr = z_g.reshape(n_sub_seq, sub, 3 * width)
    q_rows = min(sub, 512)
    n_seq = min(max(1, 1024 // sub), dilation)
    assert dilation % n_seq == 0 and sub % q_rows == 0
    assert dilation == 1 or q_rows == sub
    blocks_per_chunk = q_rows // DIL_BLOCK
    slopes = tuple(float(s) * dilation for s in _alibi_slopes()[group::ALIBI_SETS])

    def prev_block(c):
        return jnp.maximum(c * blocks_per_chunk - 1, 0)

    if dilation == 1:
        assert n_seq == 1
        out_specs = [pl.BlockSpec((None, N_HEADS, q_rows, HEAD_DIM), lambda s, c: (s, 0, c, 0)),
                     pl.BlockSpec((None, q_rows, LANES), lambda s, c: (s, c, 0))]
    else:
        out_specs = [pl.BlockSpec((None, N_HEADS, seq, HEAD_DIM),
                                  lambda s, c: (s * n_seq // dilation, 0, 0, 0)),
                     pl.BlockSpec((None, seq, LANES), lambda s, c: (s * n_seq // dilation, 0, 0))]
    return pl.pallas_call(
        functools.partial(_dilated_kernel, slopes=slopes, n_seq=n_seq, q_rows=q_rows, dilation=dilation),
        grid=(n_sub_seq // n_seq, sub // q_rows),
        in_specs=[pl.BlockSpec((n_seq, q_rows, width), lambda s, c: (s, c, 0)),
                  pl.BlockSpec((n_seq, q_rows, width), lambda s, c: (s, c, 1)),
                  pl.BlockSpec((n_seq, q_rows, width), lambda s, c: (s, c, 2)),
                  pl.BlockSpec((n_seq, DIL_BLOCK, width), lambda s, c: (s, prev_block(c), 1)),
                  pl.BlockSpec((n_seq, DIL_BLOCK, width), lambda s, c: (s, prev_block(c), 2))],
        out_specs=out_specs,
        out_shape=[jax.ShapeDtypeStruct((bsz, N_HEADS, seq, HEAD_DIM), F32),
                   jax.ShapeDtypeStruct((bsz, seq, LANES), F32)],
        scratch_shapes=[pltpu.VMEM((DIL_BLOCK + q_rows, width), BF16),
                        pltpu.VMEM((DIL_BLOCK + q_rows, width), BF16)],
        compiler_params=_params(2, 48 * MIB),
        name="dilated_attn",
    )(arr, arr, arr, arr, arr)


def _dilated_combine_kernel(o0, o1, o2, l0, l1, l2, y_ref):
    for h in range(N_HEADS):
        cols = slice(h * HEAD_DIM, (h + 1) * HEAD_DIM)
        a0 = l0[:, h:h + 1]
        a1 = l1[:, h:h + 1]
        a2 = l2[:, h:h + 1]
        m = jnp.maximum(jnp.maximum(a0, a1), a2)
        e0 = jnp.exp(a0 - m)
        e1 = jnp.exp(a1 - m)
        e2 = jnp.exp(a2 - m)
        y = (e0 * o0[h] + e1 * o1[h] + e2 * o2[h]) / (e0 + e1 + e2)
        y_ref[:, cols] = y.astype(y_ref.dtype)


def _dilated_combine(outs, lses):
    bsz, _, seq, _ = outs[0].shape
    width = BRANCH_WIDTH
    tm = 512
    per_seq = seq // tm
    o_spec = pl.BlockSpec((None, N_HEADS, tm, HEAD_DIM), lambda i: (i // per_seq, 0, i % per_seq, 0))
    l_spec = pl.BlockSpec((None, tm, LANES), lambda i: (i // per_seq, i % per_seq, 0))
    return pl.pallas_call(
        _dilated_combine_kernel,
        grid=(bsz * per_seq,),
        in_specs=[o_spec, o_spec, o_spec, l_spec, l_spec, l_spec],
        out_specs=pl.BlockSpec((tm, width), lambda i: (i, 0)),
        out_shape=jax.ShapeDtypeStruct((bsz * seq, width), BF16),
        compiler_params=_params(1, 32 * MIB),
        name="dilated_combine",
    )(*outs, *lses)


def _dilated_mixer(z_groups, bsz, seq):
    results = [_dilated_attention(z_groups[g], d, g, bsz, seq) for g, d in enumerate(DIL_DILATIONS)]
    return _dilated_combine([o for o, _ in results], [lse for _, lse in results])


def _moba_kernel(slopes_ref, q_ref, k_ref, v_ref, o_ref):
    blk = MOBA_BLOCK
    seq = k_ref.shape[0]
    n_blk = seq // blk
    slope = slopes_ref[pl.program_id(1)]

    k_mean = jnp.sum(k_ref[...].astype(F32).reshape(n_blk, blk, HEAD_DIM), axis=1) * (1.0 / blk)
    k_mean = jnp.concatenate([k_mean, jnp.zeros((HEAD_DIM - n_blk, HEAD_DIM), F32)], axis=0)
    gate_all = lax.dot_general(q_ref[...].astype(F32), k_mean, (((1,), (1,)), ((), ())),
                               precision=lax.Precision.HIGHEST, preferred_element_type=F32)
    lane = lax.broadcasted_iota(jnp.int32, (blk, HEAD_DIM), 1)
    qi = lax.broadcasted_iota(jnp.int32, (blk, blk), 0)
    ki = lax.broadcasted_iota(jnp.int32, (blk, blk), 1)
    causal = qi >= ki
    bias0 = slope * (qi - ki).astype(F32)

    for i in range(n_blk):
        rows = slice(i * blk, (i + 1) * blk)
        q = q_ref[rows, :]
        if i > MOBA_TOPK:
            gate = jnp.where(lane < i, gate_all[rows, :], NEG_INF)
            rank = jnp.zeros((blk, HEAD_DIM), jnp.int32)
            for m in range(i):
                col = gate[:, m:m + 1]
                ahead = (col > gate) | ((col == gate) & (lane > m))
                rank = rank + ahead.astype(jnp.int32)
        scores = []
        for n in range(i + 1):
            kn = k_ref[n * blk:(n + 1) * blk, :]
            s = lax.dot_general(q, kn, (((1,), (1,)), ((), ())), preferred_element_type=F32) * ATTN_SCALE
            s = s - (bias0 + slope * float((i - n) * blk))
            if n == i:
                s = jnp.where(causal, s, NEG_INF)
            elif i > MOBA_TOPK:
                s = jnp.where(rank[:, n:n + 1] < MOBA_TOPK, s, NEG_INF)
            scores.append(s)
        m_row = functools.reduce(jnp.maximum, [jnp.max(s, axis=-1, keepdims=True) for s in scores])
        den = jnp.zeros((blk, 1), F32)
        acc = jnp.zeros((blk, HEAD_DIM), F32)
        for n, s in enumerate(scores):
            p = jnp.exp(s - m_row)
            den = den + jnp.sum(p, axis=-1, keepdims=True)
            acc = acc + jnp.dot(p.astype(BF16), v_ref[n * blk:(n + 1) * blk, :],
                                preferred_element_type=F32)
        o_ref[rows, :] = (acc / den).astype(o_ref.dtype)


def _moba_mixer(z_d, bsz, seq):
    width = BRANCH_WIDTH
    z3 = z_d.reshape(bsz, seq, 3 * width)
    slopes = jnp.asarray(_alibi_slopes()[len(DIL_DILATIONS)::ALIBI_SETS])
    spec = lambda off: pl.BlockSpec((None, seq, HEAD_DIM), lambda b, h: (b, 0, off + h))
    out = pl.pallas_call(
        _moba_kernel,
        grid=(bsz, N_HEADS),
        in_specs=[pl.BlockSpec(memory_space=pltpu.SMEM), spec(0), spec(N_HEADS), spec(2 * N_HEADS)],
        out_specs=spec(0),
        out_shape=jax.ShapeDtypeStruct((bsz, seq, width), BF16),
        compiler_params=_params(2, 32 * MIB),
        name="moba",
    )(slopes, z3, z3, z3)
    return out.reshape(bsz * seq, width)


def _merge_kernel(h_ref, ya_ref, yb_ref, yc_ref, yd_ref, mw0, mw1, mw2, mw3, mb_ref, bw_ref, o_ref):
    h = h_ref[...]
    acc = None
    for b, (y_ref, mw_ref) in enumerate(zip((ya_ref, yb_ref, yc_ref, yd_ref), (mw0, mw1, mw2, mw3))):
        gate = jnp.dot(h, mw_ref[...], preferred_element_type=F32) + mb_ref[b:b + 1, :]
        proj = jnp.dot(y_ref[...], bw_ref[b], preferred_element_type=F32)
        term = jax.nn.sigmoid(gate) * proj
        acc = term if acc is None else acc + term
    o_ref[...] = acc.astype(o_ref.dtype)


def _merge(h, ys, merge_ws, merge_b, branch_w, layer, casts=()):
    t, d = h.shape
    width = BRANCH_WIDTH
    tm, tn = 1024, 256
    n_j = d // tn
    mw_spec = pl.BlockSpec((d, tn), lambda i, j: (0, j))
    y_spec = _single_buffered((tm, width), lambda i, j: (i, 0))
    vmem = (tm * d * 2 + 4 * tm * width * 2 + 2 * 4 * d * tn * 2 + 2 * 4 * width * tn * 2
            + 6 * tm * tn * 4 + VMEM_SLACK)
    (out,), cast_outs = _host_call(
        _merge_kernel, (t // tm, n_j),
        [_single_buffered((tm, d), lambda i, j: (i, 0)),
         y_spec, y_spec, y_spec, y_spec,
         mw_spec, mw_spec, mw_spec, mw_spec,
         pl.BlockSpec((None, 4, tn), lambda i, j: (layer, 0, j)),
         pl.BlockSpec((4, width, tn), lambda i, j: (0, 0, j))],
        [pl.BlockSpec((tm, tn), lambda i, j: (i, j))],
        [jax.ShapeDtypeStruct((t, d), BF16)],
        (h, *ys, *merge_ws, merge_b, branch_w), casts, vmem, "merge")
    return out, cast_outs


def _ffn_up_kernel(h_ref, wg_ref, wu_ref, cw_ref, cb_ref, o_ref):
    seq = h_ref.shape[0]
    parts = [slice(r, r + FFN_ROW_SPLIT) for r in range(0, seq, FFN_ROW_SPLIT)]

    def project(w_ref):
        return jnp.concatenate([jnp.dot(h_ref[rows, :], w_ref[...], preferred_element_type=F32)
                                for rows in parts], axis=0)

    g = project(wg_ref)
    row = lax.broadcasted_iota(jnp.int32, g.shape, 0)
    g1 = jnp.where(row >= 1, pltpu.roll(g, 1, 0), 0.0)
    g2 = jnp.where(row >= 2, pltpu.roll(g, 2, 0), 0.0)
    a = cw_ref[2:3, :] * g + cw_ref[1:2, :] * g1 + cw_ref[0:1, :] * g2 + cb_ref[...]
    u = project(wu_ref)
    o_ref[...] = (jax.nn.gelu(a) * u).astype(o_ref.dtype)


def _ffn_up(h2, wg, wu, conv_w, conv_b, layer, bsz, seq, casts=()):
    t, d = h2.shape
    f = wg.shape[1]
    tn = 256
    assert seq % FFN_ROW_SPLIT == 0
    w_spec = pl.BlockSpec((d, tn), lambda b, j: (0, j))
    vmem = seq * d * 2 + 2 * 2 * d * tn * 2 + 8 * seq * tn * 4 + VMEM_SLACK
    (out,), cast_outs = _host_call(
        _ffn_up_kernel, (bsz, f // tn),
        [_single_buffered((seq, d), lambda b, j: (b, 0)),
         w_spec, w_spec,
         pl.BlockSpec((None, 3, tn), lambda b, j: (layer, 0, j)),
         pl.BlockSpec((None, 1, tn), lambda b, j: (layer, 0, j))],
        [pl.BlockSpec((seq, tn), lambda b, j: (b, j))],
        [jax.ShapeDtypeStruct((t, f), BF16)],
        (h2, wg, wu, conv_w, conv_b), casts, vmem, "ffn_up")
    return out, cast_outs


def kernel(x, c, norm1_g, w_ada, b_ada, w_in, sgu_norm_g, sgu_w, sgu_b, pool_w, pool_scale, merge_w,
           merge_b, branch_w, out_w, norm2_g, ffn_wg, ffn_wu, conv_w, conv_b, ffn_wd, final_g):
    bsz, seq, d = x.shape
    depth = w_in.shape[0]
    width = BRANCH_WIDTH
    t = bsz * seq
    assert seq % 512 == 0 and seq // DIL_DILATIONS[-1] >= DIL_BLOCK and d == 4 * width

    n_groups = len(DIL_DILATIONS)
    w_in_runs = ([(0, 1, 3 * width, 3 * width)]
                 + [((3 + g) * width, 3, width, n_groups * width) for g in range(n_groups)]
                 + [((3 + 3 * n_groups) * width, 1, 3 * width, 3 * width)])
    w_parts = [_cast_standalone(w_in, 0, w_in_runs[0])] + [None] * (len(w_in_runs) - 1)
    pool_w_b = pool_w.astype(BF16)
    branch_w2 = branch_w.reshape(depth, 4 * width, d)
    sgu_b_t = sgu_b.transpose(0, 2, 1)
    c_pad = jnp.zeros((8, d), BF16).at[:bsz].set(c.astype(BF16))
    row3 = lambda a: a.reshape(depth, 1, a.shape[-1])

    xf = x.reshape(t, d)
    for layer in range(depth):
        mod = _ada_mod(c_pad, w_ada, row3(b_ada), layer)[:bsz].reshape(bsz * N_MOD, 1, d)

        h, *h_res = _norm_mod(xf, row3(norm1_g), layer, mod, 0, 1, seq, DIL_DILATIONS[1:])
        h_by_group = [h] + [hr.reshape(t, d) for hr in h_res]
        casts = [(branch_w2, layer)] + ([(w_in, layer, w_in_runs[1])] if w_parts[1] is None else [])
        z_ab, cast_outs = _matmul(h, w_parts[0], F32, 1024, 512, casts=casts)
        branch_w_b = cast_outs[0]
        if w_parts[1] is None:
            w_parts[1] = cast_outs[1]
        z_cd, merge_ws = [], []
        for p in range(1, len(w_parts)):
            a = h_by_group[p - 1] if p <= n_groups else h
            nxt = p + 1
            casts = [(w_in, layer, w_in_runs[nxt])] if nxt < len(w_parts) and w_parts[nxt] is None else []
            z_p, cast_outs = _matmul(a, w_parts[p], BF16, 1024, 1024, casts=casts,
                                     branch=(merge_w, layer, p - 1))
            if casts:
                w_parts[nxt] = cast_outs[0]
            z_cd.append(z_p)
            merge_ws.append(cast_outs[-1])
        z_groups, z_d = z_cd[:n_groups], z_cd[n_groups]
        y_a = _sgu(z_ab, row3(sgu_norm_g), sgu_w, sgu_b_t, layer)
        y_b = _pool(z_ab, pool_w_b, row3(pool_scale), layer, bsz, seq)
        y_c = _dilated_mixer(z_groups, bsz, seq)
        y_d = _moba_mixer(z_d, bsz, seq)
        merged, (ffn_wg_b, out_w_b) = _merge(h, (y_a, y_b, y_c, y_d), merge_ws, merge_b,
                                             branch_w_b.reshape(4, width, d), layer,
                                             casts=[(ffn_wg, layer), (out_w, layer)])
        xf, (ffn_wu_b,) = _matmul_residual(merged, out_w_b, xf, mod, 2, seq, 1024, 512,
                                           casts=[(ffn_wu, layer)], a_buffers=2)

        h2 = _norm_mod(xf, row3(norm2_g), layer, mod, 3, 4, seq)
        next_w_in = [(w_in, layer + 1, runs) for runs in w_in_runs] if layer + 1 < depth else []
        f, cast_outs = _ffn_up(h2, ffn_wg_b, ffn_wu_b, conv_w, row3(conv_b), layer, bsz, seq,
                               casts=[(ffn_wd, layer)] + next_w_in)
        ffn_wd_b = cast_outs[0]
        w_parts = list(cast_outs[1:])
        xf, _ = _matmul_residual(f, ffn_wd_b, xf, mod, 5, seq, 1024, 256)

    return _final_norm(xf, final_g.reshape(1, d)).reshape(bsz, seq, d)
```

```python
import functools

import jax
import jax.numpy as jnp
import numpy as np
from jax import lax
from jax.experimental import pallas as pl
from jax.experimental.pallas import tpu as pltpu

BF16 = jnp.bfloat16
F32 = jnp.float32

LANES = 128
HEAD_DIM = 128
N_HEADS = 8
BRANCH_WIDTH = N_HEADS * HEAD_DIM
SGU_CHUNK = 128
SGU_GROUPS = 8
POOL_WINDOWS = (2, 4, 8, 16)
POOL_GROUP_WIDTH = BRANCH_WIDTH // len(POOL_WINDOWS)
DIL_DILATIONS = (1, 4, 16)
DIL_BLOCK = 128
MOBA_BLOCK = 256
MOBA_TOPK = 3
N_ALIBI_HEADS = 32
ALIBI_SETS = 4
N_MOD = 6
NORM_EPS = 1e-6
NEG_INF = -1e30
ATTN_SCALE = HEAD_DIM ** -0.5
FFN_ROW_SPLIT = 1024

V7X_VMEM_BYTES = 64 * 2 ** 20
VMEM_LIMIT_CAP = 58 * 2 ** 20
MIB = 2 ** 20
VMEM_SLACK = 8 * MIB


def _alibi_slopes():
    n = N_ALIBI_HEADS
    return np.asarray(2.0 ** (-8.0 * np.arange(1, n + 1) / n), dtype=np.float32)


def _params(n_axes, vmem_bytes):
    limit = min(VMEM_LIMIT_CAP, max(32 * MIB, int(vmem_bytes)))
    return pltpu.CompilerParams(dimension_semantics=("arbitrary",) * n_axes,
                                vmem_limit_bytes=limit)


def _single_buffered(block_shape, index_map):
    return pl.BlockSpec(block_shape, index_map, pipeline_mode=pl.Buffered(1))


def _divisors(n, multiple_of):
    return [v for v in range(multiple_of, n + 1, multiple_of) if n % v == 0]


def _column_runs(src_cols, runs):
    return (0, 1, src_cols, src_cols) if runs is None else runs


def _cast_job(src, layer, grid, runs=None):
    _, rows, src_cols = src.shape
    col0, n_runs, run, run_stride = _column_runs(src_cols, runs)
    cols = n_runs * run
    n_steps = int(np.prod(grid))
    choices = [(rb * cb, -cb, rb, cb) for rb in _divisors(rows, 16) for cb in _divisors(run, LANES)
               if (rows // rb) * (cols // cb) <= n_steps and col0 % cb == 0 and run_stride % cb == 0]
    _, _, rb, cb = min(choices)
    n_cb = cols // cb
    per_run = run // cb
    n_blocks = (rows // rb) * n_cb

    def block(*idx):
        step = jnp.minimum(_linear_step(grid, idx), n_blocks - 1)
        return step // n_cb, step % n_cb

    def src_block(*idx):
        r, c = block(*idx)
        return layer, r, (col0 + (c // per_run) * run_stride) // cb + c % per_run

    return dict(src=src,
                in_spec=pl.BlockSpec((None, rb, cb), src_block),
                out_spec=pl.BlockSpec((rb, cb), block),
                out_shape=jax.ShapeDtypeStruct((rows, cols), BF16),
                vmem=2 * rb * cb * (4 + 2))


def _linear_step(grid, idx):
    step = idx[0]
    for extent, i in zip(grid[1:], idx[1:]):
        step = step * extent + i
    return step


def _branch_job(src, layer, branch, grid):
    _, rows, _, cols = src.shape
    n_steps = int(np.prod(grid))
    rb = min(r for r in _divisors(rows, 16) if rows // r <= n_steps)
    n_blocks = rows // rb
    return dict(src=src, layer=layer, branch=branch, rb=rb, n_blocks=n_blocks, grid=grid,
                out_spec=pl.BlockSpec((rb, cols), lambda *idx: (jnp.minimum(_linear_step(grid, idx),
                                                                            n_blocks - 1), 0)),
                out_shape=jax.ShapeDtypeStruct((rows, cols), BF16),
                scratch=[pltpu.VMEM((rb, cols), F32), pltpu.SemaphoreType.DMA(())],
                vmem=rb * cols * (4 + 2 * 2))


def _branch_copy(job, src_hbm, buf, sem):
    step = _linear_step(job["grid"], [pl.program_id(a) for a in range(len(job["grid"]))])
    row0 = jnp.minimum(step, job["n_blocks"] - 1) * job["rb"]
    return pltpu.make_async_copy(src_hbm.at[job["layer"], pl.ds(row0, job["rb"]), job["branch"], :],
                                 buf, sem)


def _hosted(body, n_in, n_out, n_cast, n_scratch, branch_job):
    n_branch = 0 if branch_job is None else 1

    def kernel_body(*refs):
        ins, refs = refs[:n_in], refs[n_in:]
        srcs, refs = refs[:n_cast], refs[n_cast:]
        branch_src, refs = refs[:n_branch], refs[n_branch:]
        outs, refs = refs[:n_out], refs[n_out:]
        dsts, refs = refs[:n_cast], refs[n_cast:]
        branch_dst, refs = refs[:n_branch], refs[n_branch:]
        scratch, branch_scratch = refs[:n_scratch], refs[n_scratch:]
        if n_branch:
            copy = _branch_copy(branch_job, branch_src[0], *branch_scratch)
            copy.start()
        body(*ins, *outs, *scratch)
        for s_ref, d_ref in zip(srcs, dsts):
            d_ref[...] = s_ref[...].astype(d_ref.dtype)
        if n_branch:
            copy.wait()
            branch_dst[0][...] = branch_scratch[0][...].astype(branch_dst[0].dtype)
    return kernel_body


def _host_call(body, grid, in_specs, out_specs, out_shapes, operands, casts, vmem, name,
               scratch_shapes=(), branch=None):
    jobs = [_cast_job(c[0], c[1], grid, *c[2:]) for c in casts]
    bjob = None if branch is None else _branch_job(*branch, grid)
    bjobs = [] if bjob is None else [bjob]
    outs = pl.pallas_call(
        _hosted(body, len(in_specs), len(out_specs), len(jobs), len(scratch_shapes), bjob),
        grid=grid,
        in_specs=(list(in_specs) + [j["in_spec"] for j in jobs]
                  + [pl.BlockSpec(memory_space=pl.ANY) for _ in bjobs]),
        out_specs=list(out_specs) + [j["out_spec"] for j in jobs + bjobs],
        out_shape=list(out_shapes) + [j["out_shape"] for j in jobs + bjobs],
        scratch_shapes=list(scratch_shapes) + [s for j in bjobs for s in j["scratch"]],
        compiler_params=_params(len(grid), vmem + sum(j["vmem"] for j in jobs + bjobs)),
        name=name,
    )(*operands, *[j["src"] for j in jobs + bjobs])
    return outs[:len(out_specs)], outs[len(out_specs):]


def _cast_kernel(s_ref, d_ref):
    d_ref[...] = s_ref[...].astype(d_ref.dtype)


def _cast_standalone(src, layer, runs=None):
    _, rows, src_cols = src.shape
    _, n_runs, run, _ = _column_runs(src_cols, runs)
    grid = (max(n_runs, rows * n_runs * run * 4 // (4 * MIB)),)
    job = _cast_job(src, layer, grid, runs)
    return pl.pallas_call(
        _cast_kernel,
        grid=grid,
        in_specs=[job["in_spec"]],
        out_specs=job["out_spec"],
        out_shape=job["out_shape"],
        compiler_params=_params(1, job["vmem"] + VMEM_SLACK),
        name="cast_weight",
    )(src)


def _ada_kernel(c_ref, w_ref, b_ref, o_ref):
    w = w_ref[...].astype(BF16)
    o_ref[...] = jnp.dot(c_ref[...], w, preferred_element_type=F32) + b_ref[...]


def _ada_mod(c_pad, w_ada, b_ada, layer):
    d = c_pad.shape[1]
    n = w_ada.shape[2]
    tn = 512
    return pl.pallas_call(
        _ada_kernel,
        grid=(n // tn,),
        in_specs=[pl.BlockSpec((8, d), lambda j: (0, 0)),
                  pl.BlockSpec((None, d, tn), lambda j: (layer, 0, j)),
                  pl.BlockSpec((None, 1, tn), lambda j: (layer, 0, j))],
        out_specs=pl.BlockSpec((8, tn), lambda j: (0, j)),
        out_shape=jax.ShapeDtypeStruct((8, n), F32),
        compiler_params=_params(1, 2 * d * tn * 4 + 3 * d * tn * 2 + 8 * MIB),
        name="ada_mod",
    )(c_pad, w_ada, b_ada)


def _norm_mod_kernel(x_ref, g_ref, sc_ref, sh_ref, o_ref):
    x = x_ref[...]
    y = x * lax.rsqrt(jnp.mean(x * x, axis=-1, keepdims=True) + NORM_EPS) * g_ref[...]
    o_ref[...] = (y * (1.0 + sc_ref[...]) + sh_ref[...]).astype(o_ref.dtype)


def _norm_mod_residue_kernel(x_ref, g_ref, sc_ref, sh_ref, o_ref, *rest):
    perm_refs, hbuf = rest[:-1], rest[-1]
    x = x_ref[...]
    y = x * lax.rsqrt(jnp.mean(x * x, axis=-1, keepdims=True) + NORM_EPS) * g_ref[...]
    h = y * (1.0 + sc_ref[...]) + sh_ref[...]
    o_ref[...] = h.astype(o_ref.dtype)
    for c in range(hbuf.shape[0]):
        cols = slice(c * LANES, (c + 1) * LANES)
        hbuf[c] = h[:, cols]
        for p_ref in perm_refs:
            d, n = p_ref.shape[0], p_ref.shape[1]
            for r in range(d):
                p_ref[r, :, cols] = hbuf[c, pl.ds(r, n, stride=d), :].astype(p_ref.dtype)


def _norm_kernel(x_ref, g_ref, o_ref):
    x = x_ref[...]
    y = x * lax.rsqrt(jnp.mean(x * x, axis=-1, keepdims=True) + NORM_EPS) * g_ref[...]
    o_ref[...] = y.astype(o_ref.dtype)


def _norm_mod(x, g, layer, mod, shift_idx, scale_idx, seq, dilations=()):
    t, d = x.shape
    tm = 256
    per_seq = seq // tm
    bsz = t // seq
    in_specs = [pl.BlockSpec((tm, d), lambda i: (i, 0)),
                pl.BlockSpec((None, 1, d), lambda i: (layer, 0, 0)),
                pl.BlockSpec((None, 1, d), lambda i: ((i // per_seq) * N_MOD + scale_idx, 0, 0)),
                pl.BlockSpec((None, 1, d), lambda i: ((i // per_seq) * N_MOD + shift_idx, 0, 0))]
    main_spec = pl.BlockSpec((tm, d), lambda i: (i, 0))
    main_shape = jax.ShapeDtypeStruct((t, d), BF16)
    if not dilations:
        return pl.pallas_call(
            _norm_mod_kernel, grid=(t // tm,), in_specs=in_specs, out_specs=main_spec,
            out_shape=main_shape, compiler_params=_params(1, 6 * tm * d * 4 + VMEM_SLACK),
            name="norm_mod",
        )(x, g, mod, mod)
    perm_specs = [pl.BlockSpec((None, dd, tm // dd, d), lambda i: (i // per_seq, 0, i % per_seq, 0))
                  for dd in dilations]
    perm_shapes = [jax.ShapeDtypeStruct((bsz, dd, seq // dd, d), BF16) for dd in dilations]
    return pl.pallas_call(
        _norm_mod_residue_kernel, grid=(t // tm,), in_specs=in_specs,
        out_specs=[main_spec] + perm_specs, out_shape=[main_shape] + perm_shapes,
        scratch_shapes=[pltpu.VMEM((d // LANES, tm, LANES), F32)],
        compiler_params=_params(1, (7 + 2 * len(dilations)) * tm * d * 4 + VMEM_SLACK),
        name="norm_mod_residue",
    )(x, g, mod, mod)


def _final_norm(x, g):
    t, d = x.shape
    tm = 256
    return pl.pallas_call(
        _norm_kernel,
        grid=(t // tm,),
        in_specs=[pl.BlockSpec((tm, d), lambda i: (i, 0)),
                  pl.BlockSpec((1, d), lambda i: (0, 0))],
        out_specs=pl.BlockSpec((tm, d), lambda i: (i, 0)),
        out_shape=jax.ShapeDtypeStruct((t, d), F32),
        compiler_params=_params(1, 6 * tm * d * 4 + 8 * MIB),
        name="final_norm",
    )(x, g)


def _mm_kernel(a_ref, w_ref, o_ref):
    o_ref[...] = jnp.dot(a_ref[...], w_ref[...], preferred_element_type=F32).astype(o_ref.dtype)


def _mm_residual_kernel(a_ref, w_ref, x_ref, g_ref, o_ref):
    acc = jnp.dot(a_ref[...], w_ref[...], preferred_element_type=F32)
    o_ref[...] = x_ref[...] + g_ref[...] * acc


def _matmul(a, w, out_dtype, tm, tn, casts=(), branch=None):
    m, k = a.shape
    n = w.shape[1]
    vmem = 2 * tm * k * 2 + 2 * k * tn * 2 + 3 * tm * tn * 4 + VMEM_SLACK
    (out,), cast_outs = _host_call(
        _mm_kernel, (m // tm, n // tn),
        [pl.BlockSpec((tm, k), lambda i, j: (i, 0)),
         pl.BlockSpec((k, tn), lambda i, j: (0, j))],
        [pl.BlockSpec((tm, tn), lambda i, j: (i, j))],
        [jax.ShapeDtypeStruct((m, n), out_dtype)],
        (a, w), casts, vmem, "proj", branch=branch)
    return out, cast_outs


def _mm_residual_ada_kernel(a_ref, w_ref, x_ref, g_ref, c_ref, wa_ref, ba_ref, o_ref, mod_ref):
    _mm_residual_kernel(a_ref, w_ref, x_ref, g_ref, o_ref)
    _ada_kernel(c_ref, wa_ref, ba_ref, mod_ref)


def _matmul_residual(a, w, x, mod, gate_idx, seq, tm, tn, casts=(), a_buffers=1, ada=None):
    m, k = a.shape
    n = w.shape[1]
    per_seq = seq // tm
    grid = (m // tm, n // tn)
    vmem = a_buffers * tm * k * 2 + 2 * k * tn * 2 + 5 * tm * tn * 4 + VMEM_SLACK
    body = _mm_residual_kernel
    in_specs = [pl.BlockSpec((tm, k), lambda i, j: (i, 0), pipeline_mode=pl.Buffered(a_buffers)),
                pl.BlockSpec((k, tn), lambda i, j: (0, j)),
                pl.BlockSpec((tm, tn), lambda i, j: (i, j)),
                pl.BlockSpec((None, 1, tn), lambda i, j: ((i // per_seq) * N_MOD + gate_idx, 0, j))]
    out_specs = [pl.BlockSpec((tm, tn), lambda i, j: (i, j))]
    out_shapes = [jax.ShapeDtypeStruct((m, n), F32)]
    operands = (a, w, x, mod)
    if ada is not None:
        c_pad, w_ada, b_ada, ada_layer = ada
        rows, d = c_pad.shape
        n_mod = w_ada.shape[2]
        ta = min(v for v in _divisors(n_mod, LANES) if n_mod // v <= grid[0] * grid[1])
        blk = lambda i, j: jnp.minimum(i * grid[1] + j, n_mod // ta - 1)
        body = _mm_residual_ada_kernel
        in_specs += [pl.BlockSpec((rows, d), lambda i, j: (0, 0)),
                     pl.BlockSpec((None, d, ta), lambda i, j: (ada_layer, 0, blk(i, j))),
                     pl.BlockSpec((None, 1, ta), lambda i, j: (ada_layer, 0, blk(i, j)))]
        out_specs += [pl.BlockSpec((rows, ta), lambda i, j: (0, blk(i, j)))]
        out_shapes += [jax.ShapeDtypeStruct((rows, n_mod), F32)]
        operands += (c_pad, w_ada, b_ada)
        vmem += 2 * d * ta * 4 + d * ta * 2
    outs, cast_outs = _host_call(body, grid, in_specs, out_specs, out_shapes, operands, casts, vmem,
                                 "proj_residual")
    return outs[0], (outs[1] if ada is not None else None), cast_outs


def _sgu_kernel(z_ref, g_ref, w_ref, bt_ref, o_ref, *, n_chunk):
    width = o_ref.shape[1]
    gw = width // SGU_GROUPS
    ti = lax.broadcasted_iota(jnp.int32, (SGU_CHUNK, SGU_CHUNK), 0)
    si = lax.broadcasted_iota(jnp.int32, (SGU_CHUNK, SGU_CHUNK), 1)
    causal = ti >= si
    w_c = [jnp.where(causal, w_ref[g], 0.0).astype(BF16) for g in range(SGU_GROUPS)]
    for c in range(n_chunk):
        rows = slice(c * SGU_CHUNK, (c + 1) * SGU_CHUNK)
        ge = jax.nn.gelu(z_ref[rows, :])
        u = ge[:, :width]
        v = ge[:, width:]
        v = v * lax.rsqrt(jnp.mean(v * v, axis=-1, keepdims=True) + NORM_EPS) * g_ref[...]
        vb = v.astype(BF16)
        for g in range(SGU_GROUPS):
            cols = slice(g * gw, (g + 1) * gw)
            mixed = jnp.dot(w_c[g], vb[:, cols], preferred_element_type=F32) + bt_ref[:, g:g + 1]
            o_ref[rows, cols] = (u[:, cols] * mixed).astype(o_ref.dtype)


def _sgu(z_ab, norm_g, w_s, b_t, layer):
    t = z_ab.shape[0]
    width = BRANCH_WIDTH
    n_chunk = 4
    tm = n_chunk * SGU_CHUNK
    return pl.pallas_call(
        functools.partial(_sgu_kernel, n_chunk=n_chunk),
        grid=(t // tm,),
        in_specs=[pl.BlockSpec((tm, 2 * width), lambda i: (i, 0)),
                  pl.BlockSpec((None, 1, width), lambda i: (layer, 0, 0)),
                  pl.BlockSpec((None, SGU_GROUPS, SGU_CHUNK, SGU_CHUNK), lambda i: (layer, 0, 0, 0)),
                  pl.BlockSpec((None, SGU_CHUNK, SGU_GROUPS), lambda i: (layer, 0, 0))],
        out_specs=pl.BlockSpec((tm, width), lambda i: (i, 0)),
        out_shape=jax.ShapeDtypeStruct((t, width), BF16),
        compiler_params=_params(1, 32 * MIB),
        name="sgu",
    )(z_ab, norm_g, w_s, b_t)


def _pool_kernel(z_ref, w_ref, sc_ref, o_ref):
    seq = z_ref.shape[0]
    pw = POOL_GROUP_WIDTH
    row = lax.broadcasted_iota(jnp.int32, (seq, pw), 0)
    pos1 = (lax.broadcasted_iota(jnp.int32, (seq, 1), 0) + 1).astype(F32)
    for g, window in enumerate(POOL_WINDOWS):
        cols = slice(g * pw, (g + 1) * pw)
        z = z_ref[:, cols]
        s = z
        shift = 1
        while shift < window:
            s = s + jnp.where(row >= shift, pltpu.roll(s, shift, 0), 0.0)
            shift *= 2
        count = jnp.minimum(pos1, float(window))
        mixed = (s / count - z).astype(BF16)
        y = jnp.dot(mixed, w_ref[g], preferred_element_type=F32) * sc_ref[:, cols]
        o_ref[:, cols] = y.astype(o_ref.dtype)


def _pool(z_ab, w_pool, scale, layer, bsz, seq):
    t = z_ab.shape[0]
    width = BRANCH_WIDTH
    pw = POOL_GROUP_WIDTH
    return pl.pallas_call(
        _pool_kernel,
        grid=(bsz,),
        in_specs=[pl.BlockSpec((seq, width), lambda b: (b, 2)),
                  pl.BlockSpec((None, len(POOL_WINDOWS), pw, pw), lambda b: (layer, 0, 0, 0)),
                  pl.BlockSpec((None, 1, width), lambda b: (layer, 0, 0))],
        out_specs=pl.BlockSpec((seq, width), lambda b: (b, 0)),
        out_shape=jax.ShapeDtypeStruct((t, width), BF16),
        compiler_params=_params(1, 48 * MIB),
        name="pool",
    )(z_ab, w_pool, scale)


def _dilated_kernel(q_ref, k_ref, v_ref, kp_ref, vp_ref, o_ref, lse_ref, kbuf, vbuf, *,
                    slopes, n_seq, q_rows, dilation):
    blk = DIL_BLOCK
    n_blk = q_rows // blk
    chunk = pl.program_id(1)
    first_residue = (pl.program_id(0) * n_seq) % dilation
    qi = lax.broadcasted_iota(jnp.int32, (blk, 2 * blk), 0)
    kk = lax.broadcasted_iota(jnp.int32, (blk, 2 * blk), 1)
    steps = qi - kk + blk
    in_window = (steps >= 0) & (steps <= blk)
    steps_f = steps.astype(F32)
    lane = lax.broadcasted_iota(jnp.int32, (blk, HEAD_DIM), 1)
    for s in range(n_seq):
        kbuf[0:blk, :] = kp_ref[s]
        kbuf[blk:, :] = k_ref[s]
        vbuf[0:blk, :] = vp_ref[s]
        vbuf[blk:, :] = v_ref[s]

        def block_body(j, carry, s=s):
            r0 = pl.multiple_of(j * blk, blk)
            key_sub = (chunk * n_blk + j - 1) * blk + kk
            valid = in_window & (key_sub >= 0)
            lse_tile = jnp.zeros((blk, HEAD_DIM), F32)
            if dilation == 1:
                out_rows = pl.ds(r0, blk)
            else:
                start = (chunk * q_rows + r0) * dilation + first_residue + s
                out_rows = pl.ds(start, blk, stride=dilation)
            for h in range(N_HEADS):
                cols = slice(h * HEAD_DIM, (h + 1) * HEAD_DIM)
                q = q_ref[s, pl.ds(r0, blk), cols]
                k = kbuf[pl.ds(r0, 2 * blk), cols]
                v = vbuf[pl.ds(r0, 2 * blk), cols]
                sc = lax.dot_general(q, k, (((1,), (1,)), ((), ())),
                                     preferred_element_type=F32) * ATTN_SCALE
                sc = jnp.where(valid, sc - slopes[h] * steps_f, NEG_INF)
                m = jnp.max(sc, axis=-1, keepdims=True)
                p = jnp.exp(sc - m)
                den = jnp.sum(p, axis=-1, keepdims=True)
                o = jnp.dot(p.astype(BF16), v, preferred_element_type=F32) / den
                o_ref[h, out_rows, :] = o
                lse_tile = jnp.where(lane == h, m + jnp.log(den), lse_tile)
            lse_ref[out_rows, :] = lse_tile
            return carry

        lax.fori_loop(0, n_blk, block_body, 0)


def _dilated_attention(z_g, dilation, group, bsz, seq):
    width = BRANCH_WIDTH
    sub = seq // dilation
    n_sub_seq = bsz * dilation
    arr = z_g.reshape(n_sub_seq, sub, 3 * width)
    q_rows = min(sub, 512)
    n_seq = min(max(1, 1024 // sub), dilation)
    assert dilation % n_seq == 0 and sub % q_rows == 0
    assert dilation == 1 or q_rows == sub
    blocks_per_chunk = q_rows // DIL_BLOCK
    slopes = tuple(float(s) * dilation for s in _alibi_slopes()[group::ALIBI_SETS])

    def prev_block(c):
        return jnp.maximum(c * blocks_per_chunk - 1, 0)

    if dilation == 1:
        assert n_seq == 1
        out_specs = [pl.BlockSpec((None, N_HEADS, q_rows, HEAD_DIM), lambda s, c: (s, 0, c, 0)),
                     pl.BlockSpec((None, q_rows, LANES), lambda s, c: (s, c, 0))]
    else:
        out_specs = [pl.BlockSpec((None, N_HEADS, seq, HEAD_DIM),
                                  lambda s, c: (s * n_seq // dilation, 0, 0, 0)),
                     pl.BlockSpec((None, seq, LANES), lambda s, c: (s * n_seq // dilation, 0, 0))]
    return pl.pallas_call(
        functools.partial(_dilated_kernel, slopes=slopes, n_seq=n_seq, q_rows=q_rows, dilation=dilation),
        grid=(n_sub_seq // n_seq, sub // q_rows),
        in_specs=[pl.BlockSpec((n_seq, q_rows, width), lambda s, c: (s, c, 0)),
                  pl.BlockSpec((n_seq, q_rows, width), lambda s, c: (s, c, 1)),
                  pl.BlockSpec((n_seq, q_rows, width), lambda s, c: (s, c, 2)),
                  pl.BlockSpec((n_seq, DIL_BLOCK, width), lambda s, c: (s, prev_block(c), 1)),
                  pl.BlockSpec((n_seq, DIL_BLOCK, width), lambda s, c: (s, prev_block(c), 2))],
        out_specs=out_specs,
        out_shape=[jax.ShapeDtypeStruct((bsz, N_HEADS, seq, HEAD_DIM), F32),
                   jax.ShapeDtypeStruct((bsz, seq, LANES), F32)],
        scratch_shapes=[pltpu.VMEM((DIL_BLOCK + q_rows, width), BF16),
                        pltpu.VMEM((DIL_BLOCK + q_rows, width), BF16)],
        compiler_params=_params(2, 48 * MIB),
        name="dilated_attn",
    )(arr, arr, arr, arr, arr)


def _dilated_combine_kernel(o0, o1, o2, l0, l1, l2, y_ref):
    for h in range(N_HEADS):
        cols = slice(h * HEAD_DIM, (h + 1) * HEAD_DIM)
        a0 = l0[:, h:h + 1]
        a1 = l1[:, h:h + 1]
        a2 = l2[:, h:h + 1]
        m = jnp.maximum(jnp.maximum(a0, a1), a2)
        e0 = jnp.exp(a0 - m)
        e1 = jnp.exp(a1 - m)
        e2 = jnp.exp(a2 - m)
        y = (e0 * o0[h] + e1 * o1[h] + e2 * o2[h]) / (e0 + e1 + e2)
        y_ref[:, cols] = y.astype(y_ref.dtype)


def _dilated_combine(outs, lses):
    bsz, _, seq, _ = outs[0].shape
    width = BRANCH_WIDTH
    tm = 512
    per_seq = seq // tm
    o_spec = pl.BlockSpec((None, N_HEADS, tm, HEAD_DIM), lambda i: (i // per_seq, 0, i % per_seq, 0))
    l_spec = pl.BlockSpec((None, tm, LANES), lambda i: (i // per_seq, i % per_seq, 0))
    return pl.pallas_call(
        _dilated_combine_kernel,
        grid=(bsz * per_seq,),
        in_specs=[o_spec, o_spec, o_spec, l_spec, l_spec, l_spec],
        out_specs=pl.BlockSpec((tm, width), lambda i: (i, 0)),
        out_shape=jax.ShapeDtypeStruct((bsz * seq, width), BF16),
        compiler_params=_params(1, 32 * MIB),
        name="dilated_combine",
    )(*outs, *lses)


def _dilated_mixer(z_groups, bsz, seq):
    results = [_dilated_attention(z_groups[g], d, g, bsz, seq) for g, d in enumerate(DIL_DILATIONS)]
    return _dilated_combine([o for o, _ in results], [lse for _, lse in results])


def _moba_kernel(slopes_ref, q_ref, k_ref, v_ref, o_ref):
    blk = MOBA_BLOCK
    seq = k_ref.shape[0]
    n_blk = seq // blk
    slope = slopes_ref[pl.program_id(1)]

    k_mean = jnp.sum(k_ref[...].astype(F32).reshape(n_blk, blk, HEAD_DIM), axis=1) * (1.0 / blk)
    k_mean = jnp.concatenate([k_mean, jnp.zeros((HEAD_DIM - n_blk, HEAD_DIM), F32)], axis=0)
    gate_all = lax.dot_general(q_ref[...].astype(F32), k_mean, (((1,), (1,)), ((), ())),
                               precision=lax.Precision.HIGHEST, preferred_element_type=F32)
    lane = lax.broadcasted_iota(jnp.int32, (blk, HEAD_DIM), 1)
    qi = lax.broadcasted_iota(jnp.int32, (blk, blk), 0)
    ki = lax.broadcasted_iota(jnp.int32, (blk, blk), 1)
    causal = qi >= ki
    bias0 = slope * (qi - ki).astype(F32)

    for i in range(n_blk):
        rows = slice(i * blk, (i + 1) * blk)
        q = q_ref[rows, :]
        if i > MOBA_TOPK:
            gate = jnp.where(lane < i, gate_all[rows, :], NEG_INF)
            rank = jnp.zeros((blk, HEAD_DIM), jnp.int32)
            for m in range(i):
                col = gate[:, m:m + 1]
                ahead = (col > gate) | ((col == gate) & (lane > m))
                rank = rank + ahead.astype(jnp.int32)
        scores = []
        for n in range(i + 1):
            kn = k_ref[n * blk:(n + 1) * blk, :]
            s = lax.dot_general(q, kn, (((1,), (1,)), ((), ())), preferred_element_type=F32) * ATTN_SCALE
            s = s - (bias0 + slope * float((i - n) * blk))
            if n == i:
                s = jnp.where(causal, s, NEG_INF)
            elif i > MOBA_TOPK:
                s = jnp.where(rank[:, n:n + 1] < MOBA_TOPK, s, NEG_INF)
            scores.append(s)
        m_row = jnp.max(functools.reduce(jnp.maximum, scores), axis=-1, keepdims=True)
        p_sum = None
        acc = jnp.zeros((blk, HEAD_DIM), F32)
        for n, s in enumerate(scores):
            p = jnp.exp(s - m_row)
            p_sum = p if p_sum is None else p_sum + p
            acc = acc + jnp.dot(p.astype(BF16), v_ref[n * blk:(n + 1) * blk, :],
                                preferred_element_type=F32)
        den = jnp.sum(p_sum, axis=-1, keepdims=True)
        o_ref[rows, :] = (acc / den).astype(o_ref.dtype)


def _moba_mixer(z_d, bsz, seq):
    width = BRANCH_WIDTH
    z3 = z_d.reshape(bsz, seq, 3 * width)
    slopes = jnp.asarray(_alibi_slopes()[len(DIL_DILATIONS)::ALIBI_SETS])
    spec = lambda off: pl.BlockSpec((None, seq, HEAD_DIM), lambda b, h: (b, 0, off + h))
    out = pl.pallas_call(
        _moba_kernel,
        grid=(bsz, N_HEADS),
        in_specs=[pl.BlockSpec(memory_space=pltpu.SMEM), spec(0), spec(N_HEADS), spec(2 * N_HEADS)],
        out_specs=spec(0),
        out_shape=jax.ShapeDtypeStruct((bsz, seq, width), BF16),
        compiler_params=_params(2, 32 * MIB),
        name="moba",
    )(slopes, z3, z3, z3)
    return out.reshape(bsz * seq, width)


def _merge_kernel(h_ref, ya_ref, yb_ref, yc_ref, yd_ref, mw0, mw1, mw2, mw3, mb_ref, bw_ref, o_ref):
    h = h_ref[...]
    acc = None
    for b, (y_ref, mw_ref) in enumerate(zip((ya_ref, yb_ref, yc_ref, yd_ref), (mw0, mw1, mw2, mw3))):
        gate = jnp.dot(h, mw_ref[...], preferred_element_type=F32) + mb_ref[b:b + 1, :]
        proj = jnp.dot(y_ref[...], bw_ref[b], preferred_element_type=F32)
        term = jax.nn.sigmoid(gate) * proj
        acc = term if acc is None else acc + term
    o_ref[...] = acc.astype(o_ref.dtype)


def _merge(h, ys, merge_ws, merge_b, branch_w, layer, casts=()):
    t, d = h.shape
    width = BRANCH_WIDTH
    tm, tn = 1024, 256
    n_j = d // tn
    mw_spec = pl.BlockSpec((d, tn), lambda i, j: (0, j))
    y_spec = _single_buffered((tm, width), lambda i, j: (i, 0))
    vmem = (tm * d * 2 + 4 * tm * width * 2 + 2 * 4 * d * tn * 2 + 2 * 4 * width * tn * 2
            + 6 * tm * tn * 4 + VMEM_SLACK)
    (out,), cast_outs = _host_call(
        _merge_kernel, (t // tm, n_j),
        [_single_buffered((tm, d), lambda i, j: (i, 0)),
         y_spec, y_spec, y_spec, y_spec,
         mw_spec, mw_spec, mw_spec, mw_spec,
         pl.BlockSpec((None, 4, tn), lambda i, j: (layer, 0, j)),
         pl.BlockSpec((4, width, tn), lambda i, j: (0, 0, j))],
        [pl.BlockSpec((tm, tn), lambda i, j: (i, j))],
        [jax.ShapeDtypeStruct((t, d), BF16)],
        (h, *ys, *merge_ws, merge_b, branch_w), casts, vmem, "merge")
    return out, cast_outs


def _ffn_up_kernel(h_ref, wg_ref, wu_ref, cw_ref, cb_ref, o_ref):
    seq = h_ref.shape[0]
    parts = [slice(r, r + FFN_ROW_SPLIT) for r in range(0, seq, FFN_ROW_SPLIT)]

    def project(w_ref):
        return jnp.concatenate([jnp.dot(h_ref[rows, :], w_ref[...], preferred_element_type=F32)
                                for rows in parts], axis=0)

    g = project(wg_ref)
    row = lax.broadcasted_iota(jnp.int32, g.shape, 0)
    g1 = jnp.where(row >= 1, pltpu.roll(g, 1, 0), 0.0)
    g2 = jnp.where(row >= 2, pltpu.roll(g, 2, 0), 0.0)
    a = cw_ref[2:3, :] * g + cw_ref[1:2, :] * g1 + cw_ref[0:1, :] * g2 + cb_ref[...]
    u = project(wu_ref)
    o_ref[...] = (jax.nn.gelu(a) * u).astype(o_ref.dtype)


def _ffn_up(h2, wg, wu, conv_w, conv_b, layer, bsz, seq, casts=()):
    t, d = h2.shape
    f = wg.shape[1]
    tn = 256
    assert seq % FFN_ROW_SPLIT == 0
    w_spec = pl.BlockSpec((d, tn), lambda b, j: (0, j))
    vmem = seq * d * 2 + 2 * 2 * d * tn * 2 + 8 * seq * tn * 4 + VMEM_SLACK
    (out,), cast_outs = _host_call(
        _ffn_up_kernel, (bsz, f // tn),
        [_single_buffered((seq, d), lambda b, j: (b, 0)),
         w_spec, w_spec,
         pl.BlockSpec((None, 3, tn), lambda b, j: (layer, 0, j)),
         pl.BlockSpec((None, 1, tn), lambda b, j: (layer, 0, j))],
        [pl.BlockSpec((seq, tn), lambda b, j: (b, j))],
        [jax.ShapeDtypeStruct((t, f), BF16)],
        (h2, wg, wu, conv_w, conv_b), casts, vmem, "ffn_up")
    return out, cast_outs


def kernel(x, c, norm1_g, w_ada, b_ada, w_in, sgu_norm_g, sgu_w, sgu_b, pool_w, pool_scale, merge_w,
           merge_b, branch_w, out_w, norm2_g, ffn_wg, ffn_wu, conv_w, conv_b, ffn_wd, final_g):
    bsz, seq, d = x.shape
    depth = w_in.shape[0]
    width = BRANCH_WIDTH
    t = bsz * seq
    assert seq % 512 == 0 and seq // DIL_DILATIONS[-1] >= DIL_BLOCK and d == 4 * width

    n_groups = len(DIL_DILATIONS)
    w_in_runs = ([(0, 1, 3 * width, 3 * width)]
                 + [((3 + g) * width, 3, width, n_groups * width) for g in range(n_groups)]
                 + [((3 + 3 * n_groups) * width, 1, 3 * width, 3 * width)])
    w_parts = [_cast_standalone(w_in, 0, w_in_runs[0])] + [None] * (len(w_in_runs) - 1)
    pool_w_b = pool_w.astype(BF16)
    branch_w2 = branch_w.reshape(depth, 4 * width, d)
    sgu_b_t = sgu_b.transpose(0, 2, 1)
    c_pad = jnp.zeros((8, d), BF16).at[:bsz].set(c.astype(BF16))
    row3 = lambda a: a.reshape(depth, 1, a.shape[-1])

    xf = x.reshape(t, d)
    mod_rows = _ada_mod(c_pad, w_ada, row3(b_ada), 0)
    for layer in range(depth):
        mod = mod_rows[:bsz].reshape(bsz * N_MOD, 1, d)

        h, *h_res = _norm_mod(xf, row3(norm1_g), layer, mod, 0, 1, seq, DIL_DILATIONS[1:])
        h_by_group = [h] + [hr.reshape(t, d) for hr in h_res]
        casts = [(branch_w2, layer)] + ([(w_in, layer, w_in_runs[1])] if w_parts[1] is None else [])
        z_ab, cast_outs = _matmul(h, w_parts[0], F32, 1024, 512, casts=casts)
        branch_w_b = cast_outs[0]
        if w_parts[1] is None:
            w_parts[1] = cast_outs[1]
        z_cd, merge_ws = [], []
        for p in range(1, len(w_parts)):
            a = h_by_group[p - 1] if p <= n_groups else h
            nxt = p + 1
            casts = [(w_in, layer, w_in_runs[nxt])] if nxt < len(w_parts) and w_parts[nxt] is None else []
            z_p, cast_outs = _matmul(a, w_parts[p], BF16, 1024, 1024, casts=casts,
                                     branch=(merge_w, layer, p - 1))
            if casts:
                w_parts[nxt] = cast_outs[0]
            z_cd.append(z_p)
            merge_ws.append(cast_outs[-1])
        z_groups, z_d = z_cd[:n_groups], z_cd[n_groups]
        y_a = _sgu(z_ab, row3(sgu_norm_g), sgu_w, sgu_b_t, layer)
        y_b = _pool(z_ab, pool_w_b, row3(pool_scale), layer, bsz, seq)
        y_c = _dilated_mixer(z_groups, bsz, seq)
        y_d = _moba_mixer(z_d, bsz, seq)
        merged, (ffn_wg_b, out_w_b) = _merge(h, (y_a, y_b, y_c, y_d), merge_ws, merge_b,
                                             branch_w_b.reshape(4, width, d), layer,
                                             casts=[(ffn_wg, layer), (out_w, layer)])
        xf, _, (ffn_wu_b,) = _matmul_residual(merged, out_w_b, xf, mod, 2, seq, 1024, 512,
                                              casts=[(ffn_wu, layer)], a_buffers=2)

        h2 = _norm_mod(xf, row3(norm2_g), layer, mod, 3, 4, seq)
        has_next = layer + 1 < depth
        next_w_in = [(w_in, layer + 1, runs) for runs in w_in_runs] if has_next else []
        f, cast_outs = _ffn_up(h2, ffn_wg_b, ffn_wu_b, conv_w, row3(conv_b), layer, bsz, seq,
                               casts=[(ffn_wd, layer)] + next_w_in)
        ffn_wd_b = cast_outs[0]
        w_parts = list(cast_outs[1:])
        xf, mod_rows, _ = _matmul_residual(f, ffn_wd_b, xf, mod, 5, seq, 1024, 256,
                                           ada=(c_pad, w_ada, row3(b_ada), layer + 1) if has_next else None)

    return _final_norm(xf, final_g.reshape(1, d)).reshape(bsz, seq, d)
```

```python
import functools

import jax
import jax.numpy as jnp
import numpy as np
from jax import lax
from jax.experimental import pallas as pl
from jax.experimental.pallas import tpu as pltpu

BF16 = jnp.bfloat16
F32 = jnp.float32

LANES = 128
HEAD_DIM = 128
N_HEADS = 8
BRANCH_WIDTH = N_HEADS * HEAD_DIM
SGU_CHUNK = 128
SGU_GROUPS = 8
POOL_WINDOWS = (2, 4, 8, 16)
POOL_GROUP_WIDTH = BRANCH_WIDTH // len(POOL_WINDOWS)
DIL_DILATIONS = (1, 4, 16)
DIL_BLOCK = 128
MOBA_BLOCK = 256
MOBA_TOPK = 3
N_ALIBI_HEADS = 32
ALIBI_SETS = 4
N_MOD = 6
NORM_EPS = 1e-6
NEG_INF = -1e30
ATTN_SCALE = HEAD_DIM ** -0.5
FFN_ROW_SPLIT = 1024

V7X_VMEM_BYTES = 64 * 2 ** 20
VMEM_LIMIT_CAP = 58 * 2 ** 20
MIB = 2 ** 20
VMEM_SLACK = 8 * MIB


def _alibi_slopes():
    n = N_ALIBI_HEADS
    return np.asarray(2.0 ** (-8.0 * np.arange(1, n + 1) / n), dtype=np.float32)


def _params(n_axes, vmem_bytes):
    limit = min(VMEM_LIMIT_CAP, max(32 * MIB, int(vmem_bytes)))
    return pltpu.CompilerParams(dimension_semantics=("arbitrary",) * n_axes,
                                vmem_limit_bytes=limit)


def _single_buffered(block_shape, index_map):
    return pl.BlockSpec(block_shape, index_map, pipeline_mode=pl.Buffered(1))


def _divisors(n, multiple_of):
    return [v for v in range(multiple_of, n + 1, multiple_of) if n % v == 0]


def _column_runs(src_cols, runs):
    return (0, 1, src_cols, src_cols) if runs is None else runs


def _cast_job(src, layer, grid, runs=None):
    _, rows, src_cols = src.shape
    col0, n_runs, run, run_stride = _column_runs(src_cols, runs)
    cols = n_runs * run
    n_steps = int(np.prod(grid))
    choices = [(rb * cb, -cb, rb, cb) for rb in _divisors(rows, 16) for cb in _divisors(run, LANES)
               if (rows // rb) * (cols // cb) <= n_steps and col0 % cb == 0 and run_stride % cb == 0]
    _, _, rb, cb = min(choices)
    n_cb = cols // cb
    per_run = run // cb
    n_blocks = (rows // rb) * n_cb

    def block(*idx):
        step = jnp.minimum(_linear_step(grid, idx), n_blocks - 1)
        return step // n_cb, step % n_cb

    def src_block(*idx):
        r, c = block(*idx)
        return layer, r, (col0 + (c // per_run) * run_stride) // cb + c % per_run

    return dict(src=src,
                in_spec=pl.BlockSpec((None, rb, cb), src_block),
                out_spec=pl.BlockSpec((rb, cb), block),
                out_shape=jax.ShapeDtypeStruct((rows, cols), BF16),
                vmem=2 * rb * cb * (4 + 2))


def _linear_step(grid, idx):
    step = idx[0]
    for extent, i in zip(grid[1:], idx[1:]):
        step = step * extent + i
    return step


def _branch_job(src, layer, branch, grid):
    _, rows, _, cols = src.shape
    n_steps = int(np.prod(grid))
    rb = min(r for r in _divisors(rows, 16) if rows // r <= n_steps)
    n_blocks = rows // rb
    return dict(src=src, layer=layer, branch=branch, rb=rb, n_blocks=n_blocks, grid=grid,
                out_spec=pl.BlockSpec((rb, cols), lambda *idx: (jnp.minimum(_linear_step(grid, idx),
                                                                            n_blocks - 1), 0)),
                out_shape=jax.ShapeDtypeStruct((rows, cols), BF16),
                scratch=[pltpu.VMEM((rb, cols), F32), pltpu.SemaphoreType.DMA(())],
                vmem=rb * cols * (4 + 2 * 2))


def _branch_copy(job, src_hbm, buf, sem):
    step = _linear_step(job["grid"], [pl.program_id(a) for a in range(len(job["grid"]))])
    row0 = jnp.minimum(step, job["n_blocks"] - 1) * job["rb"]
    return pltpu.make_async_copy(src_hbm.at[job["layer"], pl.ds(row0, job["rb"]), job["branch"], :],
                                 buf, sem)


def _hosted(body, n_in, n_out, n_cast, n_scratch, branch_job):
    n_branch = 0 if branch_job is None else 1

    def kernel_body(*refs):
        ins, refs = refs[:n_in], refs[n_in:]
        srcs, refs = refs[:n_cast], refs[n_cast:]
        branch_src, refs = refs[:n_branch], refs[n_branch:]
        outs, refs = refs[:n_out], refs[n_out:]
        dsts, refs = refs[:n_cast], refs[n_cast:]
        branch_dst, refs = refs[:n_branch], refs[n_branch:]
        scratch, branch_scratch = refs[:n_scratch], refs[n_scratch:]
        if n_branch:
            copy = _branch_copy(branch_job, branch_src[0], *branch_scratch)
            copy.start()
        body(*ins, *outs, *scratch)
        for s_ref, d_ref in zip(srcs, dsts):
            d_ref[...] = s_ref[...].astype(d_ref.dtype)
        if n_branch:
            copy.wait()
            branch_dst[0][...] = branch_scratch[0][...].astype(branch_dst[0].dtype)
    return kernel_body


def _host_call(body, grid, in_specs, out_specs, out_shapes, operands, casts, vmem, name,
               scratch_shapes=(), branch=None):
    jobs = [_cast_job(c[0], c[1], grid, *c[2:]) for c in casts]
    bjob = None if branch is None else _branch_job(*branch, grid)
    bjobs = [] if bjob is None else [bjob]
    outs = pl.pallas_call(
        _hosted(body, len(in_specs), len(out_specs), len(jobs), len(scratch_shapes), bjob),
        grid=grid,
        in_specs=(list(in_specs) + [j["in_spec"] for j in jobs]
                  + [pl.BlockSpec(memory_space=pl.ANY) for _ in bjobs]),
        out_specs=list(out_specs) + [j["out_spec"] for j in jobs + bjobs],
        out_shape=list(out_shapes) + [j["out_shape"] for j in jobs + bjobs],
        scratch_shapes=list(scratch_shapes) + [s for j in bjobs for s in j["scratch"]],
        compiler_params=_params(len(grid), vmem + sum(j["vmem"] for j in jobs + bjobs)),
        name=name,
    )(*operands, *[j["src"] for j in jobs + bjobs])
    return outs[:len(out_specs)], outs[len(out_specs):]


def _cast_kernel(s_ref, d_ref):
    d_ref[...] = s_ref[...].astype(d_ref.dtype)


def _cast_standalone(src, layer, runs=None):
    _, rows, src_cols = src.shape
    _, n_runs, run, _ = _column_runs(src_cols, runs)
    grid = (max(n_runs, rows * n_runs * run * 4 // (4 * MIB)),)
    job = _cast_job(src, layer, grid, runs)
    return pl.pallas_call(
        _cast_kernel,
        grid=grid,
        in_specs=[job["in_spec"]],
        out_specs=job["out_spec"],
        out_shape=job["out_shape"],
        compiler_params=_params(1, job["vmem"] + VMEM_SLACK),
        name="cast_weight",
    )(src)


def _ada_kernel(c_ref, w_ref, b_ref, o_ref):
    w = w_ref[...].astype(BF16)
    o_ref[...] = jnp.dot(c_ref[...], w, preferred_element_type=F32) + b_ref[...]


def _ada_mod(c_pad, w_ada, b_ada, layer):
    d = c_pad.shape[1]
    n = w_ada.shape[2]
    tn = 512
    return pl.pallas_call(
        _ada_kernel,
        grid=(n // tn,),
        in_specs=[pl.BlockSpec((8, d), lambda j: (0, 0)),
                  pl.BlockSpec((None, d, tn), lambda j: (layer, 0, j)),
                  pl.BlockSpec((None, 1, tn), lambda j: (layer, 0, j))],
        out_specs=pl.BlockSpec((8, tn), lambda j: (0, j)),
        out_shape=jax.ShapeDtypeStruct((8, n), F32),
        compiler_params=_params(1, 2 * d * tn * 4 + 3 * d * tn * 2 + 8 * MIB),
        name="ada_mod",
    )(c_pad, w_ada, b_ada)


def _norm_mod_kernel(x_ref, g_ref, sc_ref, sh_ref, o_ref):
    x = x_ref[...]
    y = x * lax.rsqrt(jnp.mean(x * x, axis=-1, keepdims=True) + NORM_EPS) * g_ref[...]
    o_ref[...] = (y * (1.0 + sc_ref[...]) + sh_ref[...]).astype(o_ref.dtype)


def _norm_mod_residue_kernel(x_ref, g_ref, sc_ref, sh_ref, o_ref, *rest):
    perm_refs, hbuf = rest[:-1], rest[-1]
    x = x_ref[...]
    y = x * lax.rsqrt(jnp.mean(x * x, axis=-1, keepdims=True) + NORM_EPS) * g_ref[...]
    h = y * (1.0 + sc_ref[...]) + sh_ref[...]
    o_ref[...] = h.astype(o_ref.dtype)
    for c in range(hbuf.shape[0]):
        cols = slice(c * LANES, (c + 1) * LANES)
        hbuf[c] = h[:, cols]
        for p_ref in perm_refs:
            d, n = p_ref.shape[0], p_ref.shape[1]
            for r in range(d):
                p_ref[r, :, cols] = hbuf[c, pl.ds(r, n, stride=d), :].astype(p_ref.dtype)


def _norm_kernel(x_ref, g_ref, o_ref):
    x = x_ref[...]
    y = x * lax.rsqrt(jnp.mean(x * x, axis=-1, keepdims=True) + NORM_EPS) * g_ref[...]
    o_ref[...] = y.astype(o_ref.dtype)


def _norm_mod(x, g, layer, mod, shift_idx, scale_idx, seq, dilations=()):
    t, d = x.shape
    tm = 256 if dilations else 512
    per_seq = seq // tm
    bsz = t // seq
    in_specs = [pl.BlockSpec((tm, d), lambda i: (i, 0)),
                pl.BlockSpec((None, 1, d), lambda i: (layer, 0, 0)),
                pl.BlockSpec((None, 1, d), lambda i: ((i // per_seq) * N_MOD + scale_idx, 0, 0)),
                pl.BlockSpec((None, 1, d), lambda i: ((i // per_seq) * N_MOD + shift_idx, 0, 0))]
    main_spec = pl.BlockSpec((tm, d), lambda i: (i, 0))
    main_shape = jax.ShapeDtypeStruct((t, d), BF16)
    if not dilations:
        return pl.pallas_call(
            _norm_mod_kernel, grid=(t // tm,), in_specs=in_specs, out_specs=main_spec,
            out_shape=main_shape, compiler_params=_params(1, 6 * tm * d * 4 + VMEM_SLACK),
            name="norm_mod",
        )(x, g, mod, mod)
    perm_specs = [pl.BlockSpec((None, dd, tm // dd, d), lambda i: (i // per_seq, 0, i % per_seq, 0))
                  for dd in dilations]
    perm_shapes = [jax.ShapeDtypeStruct((bsz, dd, seq // dd, d), BF16) for dd in dilations]
    return pl.pallas_call(
        _norm_mod_residue_kernel, grid=(t // tm,), in_specs=in_specs,
        out_specs=[main_spec] + perm_specs, out_shape=[main_shape] + perm_shapes,
        scratch_shapes=[pltpu.VMEM((d // LANES, tm, LANES), F32)],
        compiler_params=_params(1, (7 + 2 * len(dilations)) * tm * d * 4 + VMEM_SLACK),
        name="norm_mod_residue",
    )(x, g, mod, mod)


def _final_norm(x, g):
    t, d = x.shape
    tm = 512
    return pl.pallas_call(
        _norm_kernel,
        grid=(t // tm,),
        in_specs=[pl.BlockSpec((tm, d), lambda i: (i, 0)),
                  pl.BlockSpec((1, d), lambda i: (0, 0))],
        out_specs=pl.BlockSpec((tm, d), lambda i: (i, 0)),
        out_shape=jax.ShapeDtypeStruct((t, d), F32),
        compiler_params=_params(1, 6 * tm * d * 4 + 8 * MIB),
        name="final_norm",
    )(x, g)


def _mm_kernel(a_ref, w_ref, o_ref):
    o_ref[...] = jnp.dot(a_ref[...], w_ref[...], preferred_element_type=F32).astype(o_ref.dtype)


def _mm_residual_kernel(a_ref, w_ref, x_ref, g_ref, o_ref):
    acc = jnp.dot(a_ref[...], w_ref[...], preferred_element_type=F32)
    o_ref[...] = x_ref[...] + g_ref[...] * acc


def _matmul(a, w, out_dtype, tm, tn, casts=(), branch=None):
    m, k = a.shape
    n = w.shape[1]
    vmem = 2 * tm * k * 2 + 2 * k * tn * 2 + 3 * tm * tn * 4 + VMEM_SLACK
    (out,), cast_outs = _host_call(
        _mm_kernel, (m // tm, n // tn),
        [pl.BlockSpec((tm, k), lambda i, j: (i, 0)),
         pl.BlockSpec((k, tn), lambda i, j: (0, j))],
        [pl.BlockSpec((tm, tn), lambda i, j: (i, j))],
        [jax.ShapeDtypeStruct((m, n), out_dtype)],
        (a, w), casts, vmem, "proj", branch=branch)
    return out, cast_outs


def _mm_residual_ada_kernel(a_ref, w_ref, x_ref, g_ref, c_ref, wa_ref, ba_ref, o_ref, mod_ref):
    _mm_residual_kernel(a_ref, w_ref, x_ref, g_ref, o_ref)
    _ada_kernel(c_ref, wa_ref, ba_ref, mod_ref)


def _matmul_residual(a, w, x, mod, gate_idx, seq, tm, tn, casts=(), a_buffers=1, ada=None):
    m, k = a.shape
    n = w.shape[1]
    per_seq = seq // tm
    grid = (m // tm, n // tn)
    vmem = a_buffers * tm * k * 2 + 2 * k * tn * 2 + 5 * tm * tn * 4 + VMEM_SLACK
    body = _mm_residual_kernel
    in_specs = [pl.BlockSpec((tm, k), lambda i, j: (i, 0), pipeline_mode=pl.Buffered(a_buffers)),
                pl.BlockSpec((k, tn), lambda i, j: (0, j)),
                pl.BlockSpec((tm, tn), lambda i, j: (i, j)),
                pl.BlockSpec((None, 1, tn), lambda i, j: ((i // per_seq) * N_MOD + gate_idx, 0, j))]
    out_specs = [pl.BlockSpec((tm, tn), lambda i, j: (i, j))]
    out_shapes = [jax.ShapeDtypeStruct((m, n), F32)]
    operands = (a, w, x, mod)
    if ada is not None:
        c_pad, w_ada, b_ada, ada_layer = ada
        rows, d = c_pad.shape
        n_mod = w_ada.shape[2]
        ta = min(v for v in _divisors(n_mod, LANES) if n_mod // v <= grid[0] * grid[1])
        blk = lambda i, j: jnp.minimum(i * grid[1] + j, n_mod // ta - 1)
        body = _mm_residual_ada_kernel
        in_specs += [pl.BlockSpec((rows, d), lambda i, j: (0, 0)),
                     pl.BlockSpec((None, d, ta), lambda i, j: (ada_layer, 0, blk(i, j))),
                     pl.BlockSpec((None, 1, ta), lambda i, j: (ada_layer, 0, blk(i, j)))]
        out_specs += [pl.BlockSpec((rows, ta), lambda i, j: (0, blk(i, j)))]
        out_shapes += [jax.ShapeDtypeStruct((rows, n_mod), F32)]
        operands += (c_pad, w_ada, b_ada)
        vmem += 2 * d * ta * 4 + d * ta * 2
    outs, cast_outs = _host_call(body, grid, in_specs, out_specs, out_shapes, operands, casts, vmem,
                                 "proj_residual")
    return outs[0], (outs[1] if ada is not None else None), cast_outs


def _sgu_kernel(z_ref, g_ref, w_ref, bt_ref, o_ref, *, n_chunk):
    width = o_ref.shape[1]
    gw = width // SGU_GROUPS
    ti = lax.broadcasted_iota(jnp.int32, (SGU_CHUNK, SGU_CHUNK), 0)
    si = lax.broadcasted_iota(jnp.int32, (SGU_CHUNK, SGU_CHUNK), 1)
    causal = ti >= si
    w_c = [jnp.where(causal, w_ref[g], 0.0).astype(BF16) for g in range(SGU_GROUPS)]
    for c in range(n_chunk):
        rows = slice(c * SGU_CHUNK, (c + 1) * SGU_CHUNK)
        ge = jax.nn.gelu(z_ref[rows, :])
        u = ge[:, :width]
        v = ge[:, width:]
        v = v * lax.rsqrt(jnp.mean(v * v, axis=-1, keepdims=True) + NORM_EPS) * g_ref[...]
        vb = v.astype(BF16)
        for g in range(SGU_GROUPS):
            cols = slice(g * gw, (g + 1) * gw)
            mixed = jnp.dot(w_c[g], vb[:, cols], preferred_element_type=F32) + bt_ref[:, g:g + 1]
            o_ref[rows, cols] = (u[:, cols] * mixed).astype(o_ref.dtype)


def _sgu(z_ab, norm_g, w_s, b_t, layer):
    t = z_ab.shape[0]
    width = BRANCH_WIDTH
    n_chunk = 4
    tm = n_chunk * SGU_CHUNK
    return pl.pallas_call(
        functools.partial(_sgu_kernel, n_chunk=n_chunk),
        grid=(t // tm,),
        in_specs=[pl.BlockSpec((tm, 2 * width), lambda i: (i, 0)),
                  pl.BlockSpec((None, 1, width), lambda i: (layer, 0, 0)),
                  pl.BlockSpec((None, SGU_GROUPS, SGU_CHUNK, SGU_CHUNK), lambda i: (layer, 0, 0, 0)),
                  pl.BlockSpec((None, SGU_CHUNK, SGU_GROUPS), lambda i: (layer, 0, 0))],
        out_specs=pl.BlockSpec((tm, width), lambda i: (i, 0)),
        out_shape=jax.ShapeDtypeStruct((t, width), BF16),
        compiler_params=_params(1, 32 * MIB),
        name="sgu",
    )(z_ab, norm_g, w_s, b_t)


def _pool_kernel(z_ref, w_ref, sc_ref, o_ref):
    seq = z_ref.shape[0]
    pw = POOL_GROUP_WIDTH
    row = lax.broadcasted_iota(jnp.int32, (seq, pw), 0)
    pos1 = (lax.broadcasted_iota(jnp.int32, (seq, 1), 0) + 1).astype(F32)
    for g, window in enumerate(POOL_WINDOWS):
        cols = slice(g * pw, (g + 1) * pw)
        z = z_ref[:, cols]
        s = z
        shift = 1
        while shift < window:
            s = s + jnp.where(row >= shift, pltpu.roll(s, shift, 0), 0.0)
            shift *= 2
        count = jnp.minimum(pos1, float(window))
        mixed = (s / count - z).astype(BF16)
        y = jnp.dot(mixed, w_ref[g], preferred_element_type=F32) * sc_ref[:, cols]
        o_ref[:, cols] = y.astype(o_ref.dtype)


def _pool(z_ab, w_pool, scale, layer, bsz, seq):
    t = z_ab.shape[0]
    width = BRANCH_WIDTH
    pw = POOL_GROUP_WIDTH
    return pl.pallas_call(
        _pool_kernel,
        grid=(bsz,),
        in_specs=[pl.BlockSpec((seq, width), lambda b: (b, 2)),
                  pl.BlockSpec((None, len(POOL_WINDOWS), pw, pw), lambda b: (layer, 0, 0, 0)),
                  pl.BlockSpec((None, 1, width), lambda b: (layer, 0, 0))],
        out_specs=pl.BlockSpec((seq, width), lambda b: (b, 0)),
        out_shape=jax.ShapeDtypeStruct((t, width), BF16),
        compiler_params=_params(1, 48 * MIB),
        name="pool",
    )(z_ab, w_pool, scale)


def _dilated_kernel(q_ref, k_ref, v_ref, kp_ref, vp_ref, o_ref, lse_ref, kbuf, vbuf, *,
                    slopes, n_seq, q_rows, dilation):
    blk = DIL_BLOCK
    n_blk = q_rows // blk
    chunk = pl.program_id(1)
    first_residue = (pl.program_id(0) * n_seq) % dilation
    qi = lax.broadcasted_iota(jnp.int32, (blk, 2 * blk), 0)
    kk = lax.broadcasted_iota(jnp.int32, (blk, 2 * blk), 1)
    steps = qi - kk + blk
    in_window = (steps >= 0) & (steps <= blk)
    steps_f = steps.astype(F32)
    lane = lax.broadcasted_iota(jnp.int32, (blk, HEAD_DIM), 1)
    for s in range(n_seq):
        kbuf[0:blk, :] = kp_ref[s]
        kbuf[blk:, :] = k_ref[s]
        vbuf[0:blk, :] = vp_ref[s]
        vbuf[blk:, :] = v_ref[s]

        def block_body(j, carry, s=s):
            r0 = pl.multiple_of(j * blk, blk)
            key_sub = (chunk * n_blk + j - 1) * blk + kk
            valid = in_window & (key_sub >= 0)
            lse_tile = jnp.zeros((blk, HEAD_DIM), F32)
            if dilation == 1:
                out_rows = pl.ds(r0, blk)
            else:
                start = (chunk * q_rows + r0) * dilation + first_residue + s
                out_rows = pl.ds(start, blk, stride=dilation)
            for h in range(N_HEADS):
                cols = slice(h * HEAD_DIM, (h + 1) * HEAD_DIM)
                q = q_ref[s, pl.ds(r0, blk), cols]
                k = kbuf[pl.ds(r0, 2 * blk), cols]
                v = vbuf[pl.ds(r0, 2 * blk), cols]
                sc = lax.dot_general(q, k, (((1,), (1,)), ((), ())),
                                     preferred_element_type=F32) * ATTN_SCALE
                sc = jnp.where(valid, sc - slopes[h] * steps_f, NEG_INF)
                m = jnp.max(sc, axis=-1, keepdims=True)
                p = jnp.exp(sc - m)
                den = jnp.sum(p, axis=-1, keepdims=True)
                o = jnp.dot(p.astype(BF16), v, preferred_element_type=F32) / den
                o_ref[h, out_rows, :] = o
                lse_tile = jnp.where(lane == h, m + jnp.log(den), lse_tile)
            lse_ref[out_rows, :] = lse_tile
            return carry

        lax.fori_loop(0, n_blk, block_body, 0, unroll=True)


def _dilated_attention(z_g, dilation, group, bsz, seq):
    width = BRANCH_WIDTH
    sub = seq // dilation
    n_sub_seq = bsz * dilation
    arr = z_g.reshape(n_sub_seq, sub, 3 * width)
    q_rows = min(sub, 512)
    n_seq = min(max(1, 1024 // sub), dilation)
    assert dilation % n_seq == 0 and sub % q_rows == 0
    assert dilation == 1 or q_rows == sub
    blocks_per_chunk = q_rows // DIL_BLOCK
    slopes = tuple(float(s) * dilation for s in _alibi_slopes()[group::ALIBI_SETS])

    def prev_block(c):
        return jnp.maximum(c * blocks_per_chunk - 1, 0)

    if dilation == 1:
        assert n_seq == 1
        out_specs = [pl.BlockSpec((None, N_HEADS, q_rows, HEAD_DIM), lambda s, c: (s, 0, c, 0)),
                     pl.BlockSpec((None, q_rows, LANES), lambda s, c: (s, c, 0))]
    else:
        out_specs = [pl.BlockSpec((None, N_HEADS, seq, HEAD_DIM),
                                  lambda s, c: (s * n_seq // dilation, 0, 0, 0)),
                     pl.BlockSpec((None, seq, LANES), lambda s, c: (s * n_seq // dilation, 0, 0))]
    qkv_specs = [pl.BlockSpec((n_seq, q_rows, width), lambda s, c: (s, c, 0)),
                 pl.BlockSpec((n_seq, q_rows, width), lambda s, c: (s, c, 1)),
                 pl.BlockSpec((n_seq, q_rows, width), lambda s, c: (s, c, 2))]
    out_shape = [jax.ShapeDtypeStruct((bsz, N_HEADS, seq, HEAD_DIM), F32),
                 jax.ShapeDtypeStruct((bsz, seq, LANES), F32)]
    grid = (n_sub_seq // n_seq, sub // q_rows)
    return pl.pallas_call(
        functools.partial(_dilated_kernel, slopes=slopes, n_seq=n_seq, q_rows=q_rows, dilation=dilation),
        grid=grid,
        in_specs=qkv_specs + [
            pl.BlockSpec((n_seq, DIL_BLOCK, width), lambda s, c: (s, prev_block(c), 1)),
            pl.BlockSpec((n_seq, DIL_BLOCK, width), lambda s, c: (s, prev_block(c), 2))],
        out_specs=out_specs,
        out_shape=out_shape,
        scratch_shapes=[pltpu.VMEM((DIL_BLOCK + q_rows, width), BF16),
                        pltpu.VMEM((DIL_BLOCK + q_rows, width), BF16)],
        compiler_params=_params(2, 48 * MIB),
        name="dilated_attn",
    )(arr, arr, arr, arr, arr)


def _dilated_combine_kernel(o0, o1, o2, l0, l1, l2, y_ref):
    for h in range(N_HEADS):
        cols = slice(h * HEAD_DIM, (h + 1) * HEAD_DIM)
        a0 = l0[:, h:h + 1]
        a1 = l1[:, h:h + 1]
        a2 = l2[:, h:h + 1]
        m = jnp.maximum(jnp.maximum(a0, a1), a2)
        e0 = jnp.exp(a0 - m)
        e1 = jnp.exp(a1 - m)
        e2 = jnp.exp(a2 - m)
        y = (e0 * o0[h] + e1 * o1[h] + e2 * o2[h]) / (e0 + e1 + e2)
        y_ref[:, cols] = y.astype(y_ref.dtype)


def _dilated_combine(outs, lses):
    bsz, _, seq, _ = outs[0].shape
    width = BRANCH_WIDTH
    tm = 512
    per_seq = seq // tm
    o_spec = pl.BlockSpec((None, N_HEADS, tm, HEAD_DIM), lambda i: (i // per_seq, 0, i % per_seq, 0))
    l_spec = pl.BlockSpec((None, tm, LANES), lambda i: (i // per_seq, i % per_seq, 0))
    return pl.pallas_call(
        _dilated_combine_kernel,
        grid=(bsz * per_seq,),
        in_specs=[o_spec, o_spec, o_spec, l_spec, l_spec, l_spec],
        out_specs=pl.BlockSpec((tm, width), lambda i: (i, 0)),
        out_shape=jax.ShapeDtypeStruct((bsz * seq, width), BF16),
        compiler_params=_params(1, 32 * MIB),
        name="dilated_combine",
    )(*outs, *lses)


def _dilated_mixer(z_groups, bsz, seq):
    results = [_dilated_attention(z_groups[g], d, g, bsz, seq) for g, d in enumerate(DIL_DILATIONS)]
    return _dilated_combine([o for o, _ in results], [lse for _, lse in results])


def _moba_kernel(slopes_ref, q_ref, k_ref, v_ref, o_ref):
    blk = MOBA_BLOCK
    seq = k_ref.shape[0]
    n_blk = seq // blk
    slope = slopes_ref[pl.program_id(1)]

    k_mean = jnp.sum(k_ref[...].astype(F32).reshape(n_blk, blk, HEAD_DIM), axis=1) * (1.0 / blk)
    k_mean = jnp.concatenate([k_mean, jnp.zeros((HEAD_DIM - n_blk, HEAD_DIM), F32)], axis=0)
    gate_all = lax.dot_general(q_ref[...].astype(F32), k_mean, (((1,), (1,)), ((), ())),
                               precision=lax.Precision.HIGHEST, preferred_element_type=F32)
    lane = lax.broadcasted_iota(jnp.int32, (blk, HEAD_DIM), 1)
    qi = lax.broadcasted_iota(jnp.int32, (blk, blk), 0)
    ki = lax.broadcasted_iota(jnp.int32, (blk, blk), 1)
    causal = qi >= ki
    bias0 = slope * (qi - ki).astype(F32)

    for i in range(n_blk):
        rows = slice(i * blk, (i + 1) * blk)
        q = q_ref[rows, :]
        if i > MOBA_TOPK:
            gate = jnp.where(lane < i, gate_all[rows, :], NEG_INF)
            rank = jnp.zeros((blk, HEAD_DIM), jnp.int32)
            for m in range(i):
                col = gate[:, m:m + 1]
                ahead = (col > gate) | ((col == gate) & (lane > m))
                rank = rank + ahead.astype(jnp.int32)
        scores = []
        for n in range(i + 1):
            kn = k_ref[n * blk:(n + 1) * blk, :]
            s = lax.dot_general(q, kn, (((1,), (1,)), ((), ())), preferred_element_type=F32) * ATTN_SCALE
            s = s - (bias0 + slope * float((i - n) * blk))
            if n == i:
                s = jnp.where(causal, s, NEG_INF)
            elif i > MOBA_TOPK:
                s = jnp.where(rank[:, n:n + 1] < MOBA_TOPK, s, NEG_INF)
            scores.append(s)
        m_row = jnp.max(functools.reduce(jnp.maximum, scores), axis=-1, keepdims=True)
        p_sum = None
        acc = jnp.zeros((blk, HEAD_DIM), F32)
        for n, s in enumerate(scores):
            p = jnp.exp(s - m_row)
            p_sum = p if p_sum is None else p_sum + p
            acc = acc + jnp.dot(p.astype(BF16), v_ref[n * blk:(n + 1) * blk, :],
                                preferred_element_type=F32)
        den = jnp.sum(p_sum, axis=-1, keepdims=True)
        o_ref[rows, :] = (acc / den).astype(o_ref.dtype)


def _moba_mixer(z_d, bsz, seq):
    width = BRANCH_WIDTH
    z3 = z_d.reshape(bsz, seq, 3 * width)
    slopes = jnp.asarray(_alibi_slopes()[len(DIL_DILATIONS)::ALIBI_SETS])
    spec = lambda off: pl.BlockSpec((None, seq, HEAD_DIM), lambda b, h: (b, 0, off + h))
    out = pl.pallas_call(
        _moba_kernel,
        grid=(bsz, N_HEADS),
        in_specs=[pl.BlockSpec(memory_space=pltpu.SMEM), spec(0), spec(N_HEADS), spec(2 * N_HEADS)],
        out_specs=spec(0),
        out_shape=jax.ShapeDtypeStruct((bsz, seq, width), BF16),
        compiler_params=_params(2, 32 * MIB),
        name="moba",
    )(slopes, z3, z3, z3)
    return out.reshape(bsz * seq, width)


def _merge_kernel(h_ref, ya_ref, yb_ref, yc_ref, yd_ref, mw0, mw1, mw2, mw3, mb_ref, bw_ref, o_ref):
    h = h_ref[...]
    acc = None
    for b, (y_ref, mw_ref) in enumerate(zip((ya_ref, yb_ref, yc_ref, yd_ref), (mw0, mw1, mw2, mw3))):
        gate = jnp.dot(h, mw_ref[...], preferred_element_type=F32) + mb_ref[b:b + 1, :]
        proj = jnp.dot(y_ref[...], bw_ref[b], preferred_element_type=F32)
        term = jax.nn.sigmoid(gate) * proj
        acc = term if acc is None else acc + term
    o_ref[...] = acc.astype(o_ref.dtype)


def _merge(h, ys, merge_ws, merge_b, branch_w, layer, casts=()):
    t, d = h.shape
    width = BRANCH_WIDTH
    tm, tn = 1024, 256
    n_j = d // tn
    mw_spec = pl.BlockSpec((d, tn), lambda i, j: (0, j))
    y_spec = _single_buffered((tm, width), lambda i, j: (i, 0))
    vmem = (tm * d * 2 + 4 * tm * width * 2 + 2 * 4 * d * tn * 2 + 2 * 4 * width * tn * 2
            + 6 * tm * tn * 4 + VMEM_SLACK)
    (out,), cast_outs = _host_call(
        _merge_kernel, (t // tm, n_j),
        [_single_buffered((tm, d), lambda i, j: (i, 0)),
         y_spec, y_spec, y_spec, y_spec,
         mw_spec, mw_spec, mw_spec, mw_spec,
         pl.BlockSpec((None, 4, tn), lambda i, j: (layer, 0, j)),
         pl.BlockSpec((4, width, tn), lambda i, j: (0, 0, j))],
        [pl.BlockSpec((tm, tn), lambda i, j: (i, j))],
        [jax.ShapeDtypeStruct((t, d), BF16)],
        (h, *ys, *merge_ws, merge_b, branch_w), casts, vmem, "merge")
    return out, cast_outs


def _ffn_up_kernel(h_ref, wg_ref, wu_ref, cw_ref, cb_ref, o_ref):
    seq = h_ref.shape[0]
    parts = [slice(r, r + FFN_ROW_SPLIT) for r in range(0, seq, FFN_ROW_SPLIT)]

    def project(w_ref):
        return jnp.concatenate([jnp.dot(h_ref[rows, :], w_ref[...], preferred_element_type=F32)
                                for rows in parts], axis=0)

    g = project(wg_ref)
    row = lax.broadcasted_iota(jnp.int32, g.shape, 0)
    g1 = jnp.where(row >= 1, pltpu.roll(g, 1, 0), 0.0)
    g2 = jnp.where(row >= 2, pltpu.roll(g, 2, 0), 0.0)
    a = cw_ref[2:3, :] * g + cw_ref[1:2, :] * g1 + cw_ref[0:1, :] * g2 + cb_ref[...]
    u = project(wu_ref)
    o_ref[...] = (jax.nn.gelu(a) * u).astype(o_ref.dtype)


def _ffn_up(h2, wg, wu, conv_w, conv_b, layer, bsz, seq, casts=()):
    t, d = h2.shape
    f = wg.shape[1]
    tn = 256
    assert seq % FFN_ROW_SPLIT == 0
    w_spec = pl.BlockSpec((d, tn), lambda b, j: (0, j))
    vmem = seq * d * 2 + 2 * 2 * d * tn * 2 + 8 * seq * tn * 4 + VMEM_SLACK
    (out,), cast_outs = _host_call(
        _ffn_up_kernel, (bsz, f // tn),
        [_single_buffered((seq, d), lambda b, j: (b, 0)),
         w_spec, w_spec,
         pl.BlockSpec((None, 3, tn), lambda b, j: (layer, 0, j)),
         pl.BlockSpec((None, 1, tn), lambda b, j: (layer, 0, j))],
        [pl.BlockSpec((seq, tn), lambda b, j: (b, j))],
        [jax.ShapeDtypeStruct((t, f), BF16)],
        (h2, wg, wu, conv_w, conv_b), casts, vmem, "ffn_up")
    return out, cast_outs


def kernel(x, c, norm1_g, w_ada, b_ada, w_in, sgu_norm_g, sgu_w, sgu_b, pool_w, pool_scale, merge_w,
           merge_b, branch_w, out_w, norm2_g, ffn_wg, ffn_wu, conv_w, conv_b, ffn_wd, final_g):
    bsz, seq, d = x.shape
    depth = w_in.shape[0]
    width = BRANCH_WIDTH
    t = bsz * seq
    assert seq % 512 == 0 and seq // DIL_DILATIONS[-1] >= DIL_BLOCK and d == 4 * width

    n_groups = len(DIL_DILATIONS)
    w_in_runs = ([(0, 1, 3 * width, 3 * width)]
                 + [((3 + g) * width, 3, width, n_groups * width) for g in range(n_groups)]
                 + [((3 + 3 * n_groups) * width, 1, 3 * width, 3 * width)])
    w_parts = [_cast_standalone(w_in, 0, w_in_runs[0])] + [None] * (len(w_in_runs) - 1)
    pool_w_b = pool_w.astype(BF16)
    branch_w2 = branch_w.reshape(depth, 4 * width, d)
    sgu_b_t = sgu_b.transpose(0, 2, 1)
    c_pad = jnp.zeros((8, d), BF16).at[:bsz].set(c.astype(BF16))
    row3 = lambda a: a.reshape(depth, 1, a.shape[-1])

    xf = x.reshape(t, d)
    mod_rows = _ada_mod(c_pad, w_ada, row3(b_ada), 0)
    for layer in range(depth):
        mod = mod_rows[:bsz].reshape(bsz * N_MOD, 1, d)

        h, *h_res = _norm_mod(xf, row3(norm1_g), layer, mod, 0, 1, seq, DIL_DILATIONS[1:])
        h_by_group = [h] + [hr.reshape(t, d) for hr in h_res]
        casts = [(branch_w2, layer)] + ([(w_in, layer, w_in_runs[1])] if w_parts[1] is None else [])
        z_ab, cast_outs = _matmul(h, w_parts[0], F32, 1024, 512, casts=casts)
        branch_w_b = cast_outs[0]
        if w_parts[1] is None:
            w_parts[1] = cast_outs[1]
        z_cd, merge_ws = [], []
        for p in range(1, len(w_parts)):
            a = h_by_group[p - 1] if p <= n_groups else h
            nxt = p + 1
            casts = [(w_in, layer, w_in_runs[nxt])] if nxt < len(w_parts) and w_parts[nxt] is None else []
            z_p, cast_outs = _matmul(a, w_parts[p], BF16, 1024, 1024, casts=casts,
                                     branch=(merge_w, layer, p - 1))
            if casts:
                w_parts[nxt] = cast_outs[0]
            z_cd.append(z_p)
            merge_ws.append(cast_outs[-1])
        z_groups, z_d = z_cd[:n_groups], z_cd[n_groups]
        y_a = _sgu(z_ab, row3(sgu_norm_g), sgu_w, sgu_b_t, layer)
        y_b = _pool(z_ab, pool_w_b, row3(pool_scale), layer, bsz, seq)
        y_c = _dilated_mixer(z_groups, bsz, seq)
        y_d = _moba_mixer(z_d, bsz, seq)
        merged, (ffn_wg_b, out_w_b) = _merge(h, (y_a, y_b, y_c, y_d), merge_ws, merge_b,
                                             branch_w_b.reshape(4, width, d), layer,
                                             casts=[(ffn_wg, layer), (out_w, layer)])
        xf, _, (ffn_wu_b,) = _matmul_residual(merged, out_w_b, xf, mod, 2, seq, 1024, 512,
                                              casts=[(ffn_wu, layer)], a_buffers=2)

        h2 = _norm_mod(xf, row3(norm2_g), layer, mod, 3, 4, seq)
        has_next = layer + 1 < depth
        next_w_in = [(w_in, layer + 1, runs) for runs in w_in_runs] if has_next else []
        f, cast_outs = _ffn_up(h2, ffn_wg_b, ffn_wu_b, conv_w, row3(conv_b), layer, bsz, seq,
                               casts=[(ffn_wd, layer)] + next_w_in)
        ffn_wd_b = cast_outs[0]
        w_parts = list(cast_outs[1:])
        xf, mod_rows, _ = _matmul_residual(f, ffn_wd_b, xf, mod, 5, seq, 1024, 256,
                                           ada=(c_pad, w_ada, row3(b_ada), layer + 1) if has_next else None)

    return _final_norm(xf, final_g.reshape(1, d)).reshape(bsz, seq, d)
```

```python
import functools

import jax
import jax.numpy as jnp
import numpy as np
from jax import lax
from jax.experimental import pallas as pl
from jax.experimental.pallas import tpu as pltpu

BF16 = jnp.bfloat16
F32 = jnp.float32

LANES = 128
HEAD_DIM = 128
N_HEADS = 8
BRANCH_WIDTH = N_HEADS * HEAD_DIM
SGU_CHUNK = 128
SGU_GROUPS = 8
POOL_WINDOWS = (2, 4, 8, 16)
POOL_GROUP_WIDTH = BRANCH_WIDTH // len(POOL_WINDOWS)
DIL_DILATIONS = (1, 4, 16)
DIL_BLOCK = 128
MOBA_BLOCK = 256
MOBA_TOPK = 3
N_ALIBI_HEADS = 32
ALIBI_SETS = 4
N_MOD = 6
NORM_EPS = 1e-6
NEG_INF = -1e30
ATTN_SCALE = HEAD_DIM ** -0.5
FFN_ROW_SPLIT = 1024

V7X_VMEM_BYTES = 64 * 2 ** 20
VMEM_LIMIT_CAP = 58 * 2 ** 20
MIB = 2 ** 20
VMEM_SLACK = 8 * MIB


def _alibi_slopes():
    n = N_ALIBI_HEADS
    return np.asarray(2.0 ** (-8.0 * np.arange(1, n + 1) / n), dtype=np.float32)


def _params(n_axes, vmem_bytes):
    limit = min(VMEM_LIMIT_CAP, max(32 * MIB, int(vmem_bytes)))
    return pltpu.CompilerParams(dimension_semantics=("arbitrary",) * n_axes,
                                vmem_limit_bytes=limit)


def _single_buffered(block_shape, index_map):
    return pl.BlockSpec(block_shape, index_map, pipeline_mode=pl.Buffered(1))


def _divisors(n, multiple_of):
    return [v for v in range(multiple_of, n + 1, multiple_of) if n % v == 0]


def _column_runs(src_cols, runs):
    return (0, 1, src_cols, src_cols) if runs is None else runs


def _cast_job(src, layer, grid, runs=None):
    _, rows, src_cols = src.shape
    col0, n_runs, run, run_stride = _column_runs(src_cols, runs)
    cols = n_runs * run
    n_steps = int(np.prod(grid))
    choices = [(rb * cb, -cb, rb, cb) for rb in _divisors(rows, 16) for cb in _divisors(run, LANES)
               if (rows // rb) * (cols // cb) <= n_steps and col0 % cb == 0 and run_stride % cb == 0]
    _, _, rb, cb = min(choices)
    n_cb = cols // cb
    per_run = run // cb
    n_blocks = (rows // rb) * n_cb

    def block(*idx):
        step = jnp.minimum(_linear_step(grid, idx), n_blocks - 1)
        return step // n_cb, step % n_cb

    def src_block(*idx):
        r, c = block(*idx)
        return layer, r, (col0 + (c // per_run) * run_stride) // cb + c % per_run

    return dict(src=src,
                in_spec=pl.BlockSpec((None, rb, cb), src_block),
                out_spec=pl.BlockSpec((rb, cb), block),
                out_shape=jax.ShapeDtypeStruct((rows, cols), BF16),
                vmem=2 * rb * cb * (4 + 2))


def _linear_step(grid, idx):
    step = idx[0]
    for extent, i in zip(grid[1:], idx[1:]):
        step = step * extent + i
    return step


def _branch_job(src, layer, branch, grid):
    _, rows, _, cols = src.shape
    n_steps = int(np.prod(grid))
    rb = min(r for r in _divisors(rows, 16) if rows // r <= n_steps)
    n_blocks = rows // rb
    return dict(src=src, layer=layer, branch=branch, rb=rb, n_blocks=n_blocks, grid=grid,
                out_spec=pl.BlockSpec((rb, cols), lambda *idx: (jnp.minimum(_linear_step(grid, idx),
                                                                            n_blocks - 1), 0)),
                out_shape=jax.ShapeDtypeStruct((rows, cols), BF16),
                scratch=[pltpu.VMEM((rb, cols), F32), pltpu.SemaphoreType.DMA(())],
                vmem=rb * cols * (4 + 2 * 2))


def _branch_copy(job, src_hbm, buf, sem):
    step = _linear_step(job["grid"], [pl.program_id(a) for a in range(len(job["grid"]))])
    row0 = jnp.minimum(step, job["n_blocks"] - 1) * job["rb"]
    return pltpu.make_async_copy(src_hbm.at[job["layer"], pl.ds(row0, job["rb"]), job["branch"], :],
                                 buf, sem)


def _hosted(body, n_in, n_out, n_cast, n_scratch, branch_job):
    n_branch = 0 if branch_job is None else 1

    def kernel_body(*refs):
        ins, refs = refs[:n_in], refs[n_in:]
        srcs, refs = refs[:n_cast], refs[n_cast:]
        branch_src, refs = refs[:n_branch], refs[n_branch:]
        outs, refs = refs[:n_out], refs[n_out:]
        dsts, refs = refs[:n_cast], refs[n_cast:]
        branch_dst, refs = refs[:n_branch], refs[n_branch:]
        scratch, branch_scratch = refs[:n_scratch], refs[n_scratch:]
        if n_branch:
            copy = _branch_copy(branch_job, branch_src[0], *branch_scratch)
            copy.start()
        body(*ins, *outs, *scratch)
        for s_ref, d_ref in zip(srcs, dsts):
            d_ref[...] = s_ref[...].astype(d_ref.dtype)
        if n_branch:
            copy.wait()
            branch_dst[0][...] = branch_scratch[0][...].astype(branch_dst[0].dtype)
    return kernel_body


def _host_call(body, grid, in_specs, out_specs, out_shapes, operands, casts, vmem, name,
               scratch_shapes=(), branch=None):
    jobs = [_cast_job(c[0], c[1], grid, *c[2:]) for c in casts]
    bjob = None if branch is None else _branch_job(*branch, grid)
    bjobs = [] if bjob is None else [bjob]
    outs = pl.pallas_call(
        _hosted(body, len(in_specs), len(out_specs), len(jobs), len(scratch_shapes), bjob),
        grid=grid,
        in_specs=(list(in_specs) + [j["in_spec"] for j in jobs]
                  + [pl.BlockSpec(memory_space=pl.ANY) for _ in bjobs]),
        out_specs=list(out_specs) + [j["out_spec"] for j in jobs + bjobs],
        out_shape=list(out_shapes) + [j["out_shape"] for j in jobs + bjobs],
        scratch_shapes=list(scratch_shapes) + [s for j in bjobs for s in j["scratch"]],
        compiler_params=_params(len(grid), vmem + sum(j["vmem"] for j in jobs + bjobs)),
        name=name,
    )(*operands, *[j["src"] for j in jobs + bjobs])
    return outs[:len(out_specs)], outs[len(out_specs):]


def _cast_kernel(s_ref, d_ref):
    d_ref[...] = s_ref[...].astype(d_ref.dtype)


def _cast_standalone(src, layer, runs=None):
    _, rows, src_cols = src.shape
    _, n_runs, run, _ = _column_runs(src_cols, runs)
    grid = (max(n_runs, rows * n_runs * run * 4 // (4 * MIB)),)
    job = _cast_job(src, layer, grid, runs)
    return pl.pallas_call(
        _cast_kernel,
        grid=grid,
        in_specs=[job["in_spec"]],
        out_specs=job["out_spec"],
        out_shape=job["out_shape"],
        compiler_params=_params(1, job["vmem"] + VMEM_SLACK),
        name="cast_weight",
    )(src)


def _ada_kernel(c_ref, w_ref, b_ref, o_ref):
    w = w_ref[...].astype(BF16)
    o_ref[...] = jnp.dot(c_ref[...], w, preferred_element_type=F32) + b_ref[...]


def _ada_mod(c_pad, w_ada, b_ada, layer):
    d = c_pad.shape[1]
    n = w_ada.shape[2]
    tn = 512
    return pl.pallas_call(
        _ada_kernel,
        grid=(n // tn,),
        in_specs=[pl.BlockSpec((8, d), lambda j: (0, 0)),
                  pl.BlockSpec((None, d, tn), lambda j: (layer, 0, j)),
                  pl.BlockSpec((None, 1, tn), lambda j: (layer, 0, j))],
        out_specs=pl.BlockSpec((8, tn), lambda j: (0, j)),
        out_shape=jax.ShapeDtypeStruct((8, n), F32),
        compiler_params=_params(1, 2 * d * tn * 4 + 3 * d * tn * 2 + 8 * MIB),
        name="ada_mod",
    )(c_pad, w_ada, b_ada)


def _norm_mod_kernel(x_ref, g_ref, sc_ref, sh_ref, o_ref):
    x = x_ref[...]
    y = x * lax.rsqrt(jnp.mean(x * x, axis=-1, keepdims=True) + NORM_EPS) * g_ref[...]
    o_ref[...] = (y * (1.0 + sc_ref[...]) + sh_ref[...]).astype(o_ref.dtype)


def _norm_mod_residue_kernel(x_ref, g_ref, sc_ref, sh_ref, o_ref, *rest):
    perm_refs, hbuf = rest[:-1], rest[-1]
    x = x_ref[...]
    y = x * lax.rsqrt(jnp.mean(x * x, axis=-1, keepdims=True) + NORM_EPS) * g_ref[...]
    h = y * (1.0 + sc_ref[...]) + sh_ref[...]
    o_ref[...] = h.astype(o_ref.dtype)
    for c in range(hbuf.shape[0]):
        cols = slice(c * LANES, (c + 1) * LANES)
        hbuf[c] = h[:, cols]
        for p_ref in perm_refs:
            d, n = p_ref.shape[0], p_ref.shape[1]
            for r in range(d):
                p_ref[r, :, cols] = hbuf[c, pl.ds(r, n, stride=d), :].astype(p_ref.dtype)


def _norm_kernel(x_ref, g_ref, o_ref):
    x = x_ref[...]
    y = x * lax.rsqrt(jnp.mean(x * x, axis=-1, keepdims=True) + NORM_EPS) * g_ref[...]
    o_ref[...] = y.astype(o_ref.dtype)


def _norm_mod(x, g, layer, mod, shift_idx, scale_idx, seq, dilations=()):
    t, d = x.shape
    tm = 256 if dilations else 512
    per_seq = seq // tm
    bsz = t // seq
    in_specs = [pl.BlockSpec((tm, d), lambda i: (i, 0)),
                pl.BlockSpec((None, 1, d), lambda i: (layer, 0, 0)),
                pl.BlockSpec((None, 1, d), lambda i: ((i // per_seq) * N_MOD + scale_idx, 0, 0)),
                pl.BlockSpec((None, 1, d), lambda i: ((i // per_seq) * N_MOD + shift_idx, 0, 0))]
    main_spec = pl.BlockSpec((tm, d), lambda i: (i, 0))
    main_shape = jax.ShapeDtypeStruct((t, d), BF16)
    if not dilations:
        return pl.pallas_call(
            _norm_mod_kernel, grid=(t // tm,), in_specs=in_specs, out_specs=main_spec,
            out_shape=main_shape, compiler_params=_params(1, 6 * tm * d * 4 + VMEM_SLACK),
            name="norm_mod",
        )(x, g, mod, mod)
    perm_specs = [pl.BlockSpec((None, dd, tm // dd, d), lambda i: (i // per_seq, 0, i % per_seq, 0))
                  for dd in dilations]
    perm_shapes = [jax.ShapeDtypeStruct((bsz, dd, seq // dd, d), BF16) for dd in dilations]
    return pl.pallas_call(
        _norm_mod_residue_kernel, grid=(t // tm,), in_specs=in_specs,
        out_specs=[main_spec] + perm_specs, out_shape=[main_shape] + perm_shapes,
        scratch_shapes=[pltpu.VMEM((d // LANES, tm, LANES), F32)],
        compiler_params=_params(1, (7 + 2 * len(dilations)) * tm * d * 4 + VMEM_SLACK),
        name="norm_mod_residue",
    )(x, g, mod, mod)


def _final_norm(x, g):
    t, d = x.shape
    tm = 512
    return pl.pallas_call(
        _norm_kernel,
        grid=(t // tm,),
        in_specs=[pl.BlockSpec((tm, d), lambda i: (i, 0)),
                  pl.BlockSpec((1, d), lambda i: (0, 0))],
        out_specs=pl.BlockSpec((tm, d), lambda i: (i, 0)),
        out_shape=jax.ShapeDtypeStruct((t, d), F32),
        compiler_params=_params(1, 6 * tm * d * 4 + 8 * MIB),
        name="final_norm",
    )(x, g)


def _mm_kernel(a_ref, w_ref, o_ref):
    o_ref[...] = jnp.dot(a_ref[...], w_ref[...], preferred_element_type=F32).astype(o_ref.dtype)


def _mm_residual_kernel(a_ref, w_ref, x_ref, g_ref, o_ref):
    acc = jnp.dot(a_ref[...], w_ref[...], preferred_element_type=F32)
    o_ref[...] = x_ref[...] + g_ref[...] * acc


def _matmul(a, w, out_dtype, tm, tn, casts=(), branch=None):
    m, k = a.shape
    n = w.shape[1]
    vmem = 2 * tm * k * 2 + 2 * k * tn * 2 + 3 * tm * tn * 4 + VMEM_SLACK
    (out,), cast_outs = _host_call(
        _mm_kernel, (m // tm, n // tn),
        [pl.BlockSpec((tm, k), lambda i, j: (i, 0)),
         pl.BlockSpec((k, tn), lambda i, j: (0, j))],
        [pl.BlockSpec((tm, tn), lambda i, j: (i, j))],
        [jax.ShapeDtypeStruct((m, n), out_dtype)],
        (a, w), casts, vmem, "proj", branch=branch)
    return out, cast_outs


def _mm_residual_ada_kernel(a_ref, w_ref, x_ref, g_ref, c_ref, wa_ref, ba_ref, o_ref, mod_ref):
    _mm_residual_kernel(a_ref, w_ref, x_ref, g_ref, o_ref)
    _ada_kernel(c_ref, wa_ref, ba_ref, mod_ref)


def _matmul_residual(a, w, x, mod, gate_idx, seq, tm, tn, casts=(), a_buffers=1, ada=None):
    m, k = a.shape
    n = w.shape[1]
    per_seq = seq // tm
    grid = (m // tm, n // tn)
    vmem = a_buffers * tm * k * 2 + 2 * k * tn * 2 + 5 * tm * tn * 4 + VMEM_SLACK
    body = _mm_residual_kernel
    in_specs = [pl.BlockSpec((tm, k), lambda i, j: (i, 0), pipeline_mode=pl.Buffered(a_buffers)),
                pl.BlockSpec((k, tn), lambda i, j: (0, j)),
                pl.BlockSpec((tm, tn), lambda i, j: (i, j)),
                pl.BlockSpec((None, 1, tn), lambda i, j: ((i // per_seq) * N_MOD + gate_idx, 0, j))]
    out_specs = [pl.BlockSpec((tm, tn), lambda i, j: (i, j))]
    out_shapes = [jax.ShapeDtypeStruct((m, n), F32)]
    operands = (a, w, x, mod)
    if ada is not None:
        c_pad, w_ada, b_ada, ada_layer = ada
        rows, d = c_pad.shape
        n_mod = w_ada.shape[2]
        ta = min(v for v in _divisors(n_mod, LANES) if n_mod // v <= grid[0] * grid[1])
        blk = lambda i, j: jnp.minimum(i * grid[1] + j, n_mod // ta - 1)
        body = _mm_residual_ada_kernel
        in_specs += [pl.BlockSpec((rows, d), lambda i, j: (0, 0)),
                     pl.BlockSpec((None, d, ta), lambda i, j: (ada_layer, 0, blk(i, j))),
                     pl.BlockSpec((None, 1, ta), lambda i, j: (ada_layer, 0, blk(i, j)))]
        out_specs += [pl.BlockSpec((rows, ta), lambda i, j: (0, blk(i, j)))]
        out_shapes += [jax.ShapeDtypeStruct((rows, n_mod), F32)]
        operands += (c_pad, w_ada, b_ada)
        vmem += 2 * d * ta * 4 + d * ta * 2
    outs, cast_outs = _host_call(body, grid, in_specs, out_specs, out_shapes, operands, casts, vmem,
                                 "proj_residual")
    return outs[0], (outs[1] if ada is not None else None), cast_outs


def _sgu_kernel(z_ref, g_ref, w_ref, bt_ref, o_ref, *, n_chunk):
    width = o_ref.shape[1]
    gw = width // SGU_GROUPS
    ti = lax.broadcasted_iota(jnp.int32, (SGU_CHUNK, SGU_CHUNK), 0)
    si = lax.broadcasted_iota(jnp.int32, (SGU_CHUNK, SGU_CHUNK), 1)
    causal = ti >= si
    w_c = [jnp.where(causal, w_ref[g], 0.0).astype(BF16) for g in range(SGU_GROUPS)]
    for c in range(n_chunk):
        rows = slice(c * SGU_CHUNK, (c + 1) * SGU_CHUNK)
        ge = jax.nn.gelu(z_ref[rows, :])
        u = ge[:, :width]
        v = ge[:, width:]
        v = v * lax.rsqrt(jnp.mean(v * v, axis=-1, keepdims=True) + NORM_EPS) * g_ref[...]
        vb = v.astype(BF16)
        for g in range(SGU_GROUPS):
            cols = slice(g * gw, (g + 1) * gw)
            mixed = jnp.dot(w_c[g], vb[:, cols], preferred_element_type=F32) + bt_ref[:, g:g + 1]
            o_ref[rows, cols] = (u[:, cols] * mixed).astype(o_ref.dtype)


def _sgu(z_ab, norm_g, w_s, b_t, layer):
    t = z_ab.shape[0]
    width = BRANCH_WIDTH
    n_chunk = 4
    tm = n_chunk * SGU_CHUNK
    return pl.pallas_call(
        functools.partial(_sgu_kernel, n_chunk=n_chunk),
        grid=(t // tm,),
        in_specs=[pl.BlockSpec((tm, 2 * width), lambda i: (i, 0)),
                  pl.BlockSpec((None, 1, width), lambda i: (layer, 0, 0)),
                  pl.BlockSpec((None, SGU_GROUPS, SGU_CHUNK, SGU_CHUNK), lambda i: (layer, 0, 0, 0)),
                  pl.BlockSpec((None, SGU_CHUNK, SGU_GROUPS), lambda i: (layer, 0, 0))],
        out_specs=pl.BlockSpec((tm, width), lambda i: (i, 0)),
        out_shape=jax.ShapeDtypeStruct((t, width), BF16),
        compiler_params=_params(1, 32 * MIB),
        name="sgu",
    )(z_ab, norm_g, w_s, b_t)


def _pool_kernel(z_ref, w_ref, sc_ref, o_ref):
    seq = z_ref.shape[0]
    pw = POOL_GROUP_WIDTH
    row = lax.broadcasted_iota(jnp.int32, (seq, pw), 0)
    pos1 = (lax.broadcasted_iota(jnp.int32, (seq, 1), 0) + 1).astype(F32)
    for g, window in enumerate(POOL_WINDOWS):
        cols = slice(g * pw, (g + 1) * pw)
        z = z_ref[:, cols]
        s = z
        shift = 1
        while shift < window:
            s = s + jnp.where(row >= shift, pltpu.roll(s, shift, 0), 0.0)
            shift *= 2
        count = jnp.minimum(pos1, float(window))
        mixed = (s / count - z).astype(BF16)
        y = jnp.dot(mixed, w_ref[g], preferred_element_type=F32) * sc_ref[:, cols]
        o_ref[:, cols] = y.astype(o_ref.dtype)


def _pool(z_ab, w_pool, scale, layer, bsz, seq):
    t = z_ab.shape[0]
    width = BRANCH_WIDTH
    pw = POOL_GROUP_WIDTH
    return pl.pallas_call(
        _pool_kernel,
        grid=(bsz,),
        in_specs=[pl.BlockSpec((seq, width), lambda b: (b, 2)),
                  pl.BlockSpec((None, len(POOL_WINDOWS), pw, pw), lambda b: (layer, 0, 0, 0)),
                  pl.BlockSpec((None, 1, width), lambda b: (layer, 0, 0))],
        out_specs=pl.BlockSpec((seq, width), lambda b: (b, 0)),
        out_shape=jax.ShapeDtypeStruct((t, width), BF16),
        compiler_params=_params(1, 48 * MIB),
        name="pool",
    )(z_ab, w_pool, scale)


def _dilated_kernel(q_ref, k_ref, v_ref, kp_ref, vp_ref, o_ref, lse_ref, kbuf, vbuf, *,
                    slopes, n_seq, q_rows, dilation):
    blk = DIL_BLOCK
    n_blk = q_rows // blk
    chunk = pl.program_id(1)
    first_residue = (pl.program_id(0) * n_seq) % dilation
    qi = lax.broadcasted_iota(jnp.int32, (blk, 2 * blk), 0)
    kk = lax.broadcasted_iota(jnp.int32, (blk, 2 * blk), 1)
    steps = qi - kk + blk
    in_window = (steps >= 0) & (steps <= blk)
    steps_f = steps.astype(F32)
    lane = lax.broadcasted_iota(jnp.int32, (blk, HEAD_DIM), 1)
    for s in range(n_seq):
        kbuf[0:blk, :] = kp_ref[s]
        kbuf[blk:, :] = k_ref[s]
        vbuf[0:blk, :] = vp_ref[s]
        vbuf[blk:, :] = v_ref[s]

        def block_body(j, carry, s=s):
            r0 = pl.multiple_of(j * blk, blk)
            key_sub = (chunk * n_blk + j - 1) * blk + kk
            valid = in_window & (key_sub >= 0)
            lse_tile = jnp.zeros((blk, HEAD_DIM), F32)
            if dilation == 1:
                out_rows = pl.ds(r0, blk)
            else:
                start = (chunk * q_rows + r0) * dilation + first_residue + s
                out_rows = pl.ds(start, blk, stride=dilation)
            for h in range(N_HEADS):
                cols = slice(h * HEAD_DIM, (h + 1) * HEAD_DIM)
                q = q_ref[s, pl.ds(r0, blk), cols]
                k = kbuf[pl.ds(r0, 2 * blk), cols]
                v = vbuf[pl.ds(r0, 2 * blk), cols]
                sc = lax.dot_general(q, k, (((1,), (1,)), ((), ())),
                                     preferred_element_type=F32) * ATTN_SCALE
                sc = jnp.where(valid, sc - slopes[h] * steps_f, NEG_INF)
                m = jnp.max(sc, axis=-1, keepdims=True)
                p = jnp.exp(sc - m)
                den = jnp.sum(p, axis=-1, keepdims=True)
                o = jnp.dot(p.astype(BF16), v, preferred_element_type=F32) / den
                o_ref[h, out_rows, :] = o
                lse_tile = jnp.where(lane == h, m + jnp.log(den), lse_tile)
            lse_ref[out_rows, :] = lse_tile
            return carry

        lax.fori_loop(0, n_blk, block_body, 0, unroll=True)


def _dilated_attention(z_g, dilation, group, bsz, seq):
    width = BRANCH_WIDTH
    sub = seq // dilation
    n_sub_seq = bsz * dilation
    arr = z_g.reshape(n_sub_seq, sub, 3 * width)
    q_rows = min(sub, 512)
    n_seq = min(max(1, 1024 // sub), dilation)
    assert dilation % n_seq == 0 and sub % q_rows == 0
    assert dilation == 1 or q_rows == sub
    blocks_per_chunk = q_rows // DIL_BLOCK
    slopes = tuple(float(s) * dilation for s in _alibi_slopes()[group::ALIBI_SETS])

    def prev_block(c):
        return jnp.maximum(c * blocks_per_chunk - 1, 0)

    if dilation == 1:
        assert n_seq == 1
        out_specs = [pl.BlockSpec((None, N_HEADS, q_rows, HEAD_DIM), lambda s, c: (s, 0, c, 0)),
                     pl.BlockSpec((None, q_rows, LANES), lambda s, c: (s, c, 0))]
    else:
        out_specs = [pl.BlockSpec((None, N_HEADS, seq, HEAD_DIM),
                                  lambda s, c: (s * n_seq // dilation, 0, 0, 0)),
                     pl.BlockSpec((None, seq, LANES), lambda s, c: (s * n_seq // dilation, 0, 0))]
    qkv_specs = [pl.BlockSpec((n_seq, q_rows, width), lambda s, c: (s, c, 0)),
                 pl.BlockSpec((n_seq, q_rows, width), lambda s, c: (s, c, 1)),
                 pl.BlockSpec((n_seq, q_rows, width), lambda s, c: (s, c, 2))]
    out_shape = [jax.ShapeDtypeStruct((bsz, N_HEADS, seq, HEAD_DIM), F32),
                 jax.ShapeDtypeStruct((bsz, seq, LANES), F32)]
    grid = (n_sub_seq // n_seq, sub // q_rows)
    return pl.pallas_call(
        functools.partial(_dilated_kernel, slopes=slopes, n_seq=n_seq, q_rows=q_rows, dilation=dilation),
        grid=grid,
        in_specs=qkv_specs + [
            pl.BlockSpec((n_seq, DIL_BLOCK, width), lambda s, c: (s, prev_block(c), 1)),
            pl.BlockSpec((n_seq, DIL_BLOCK, width), lambda s, c: (s, prev_block(c), 2))],
        out_specs=out_specs,
        out_shape=out_shape,
        scratch_shapes=[pltpu.VMEM((DIL_BLOCK + q_rows, width), BF16),
                        pltpu.VMEM((DIL_BLOCK + q_rows, width), BF16)],
        compiler_params=_params(2, 48 * MIB),
        name="dilated_attn",
    )(arr, arr, arr, arr, arr)


def _dilated_combine_kernel(o0, o1, o2, l0, l1, l2, y_ref):
    for h in range(N_HEADS):
        cols = slice(h * HEAD_DIM, (h + 1) * HEAD_DIM)
        a0 = l0[:, h:h + 1]
        a1 = l1[:, h:h + 1]
        a2 = l2[:, h:h + 1]
        m = jnp.maximum(jnp.maximum(a0, a1), a2)
        e0 = jnp.exp(a0 - m)
        e1 = jnp.exp(a1 - m)
        e2 = jnp.exp(a2 - m)
        y = (e0 * o0[h] + e1 * o1[h] + e2 * o2[h]) / (e0 + e1 + e2)
        y_ref[:, cols] = y.astype(y_ref.dtype)


def _dilated_combine(outs, lses):
    bsz, _, seq, _ = outs[0].shape
    width = BRANCH_WIDTH
    tm = 512
    per_seq = seq // tm
    o_spec = pl.BlockSpec((None, N_HEADS, tm, HEAD_DIM), lambda i: (i // per_seq, 0, i % per_seq, 0))
    l_spec = pl.BlockSpec((None, tm, LANES), lambda i: (i // per_seq, i % per_seq, 0))
    return pl.pallas_call(
        _dilated_combine_kernel,
        grid=(bsz * per_seq,),
        in_specs=[o_spec, o_spec, o_spec, l_spec, l_spec, l_spec],
        out_specs=pl.BlockSpec((tm, width), lambda i: (i, 0)),
        out_shape=jax.ShapeDtypeStruct((bsz * seq, width), BF16),
        compiler_params=_params(1, 32 * MIB),
        name="dilated_combine",
    )(*outs, *lses)


def _dilated_mixer(z_groups, bsz, seq):
    results = [_dilated_attention(z_groups[g], d, g, bsz, seq) for g, d in enumerate(DIL_DILATIONS)]
    return _dilated_combine([o for o, _ in results], [lse for _, lse in results])


def _moba_kernel(slopes_ref, q_ref, k_ref, v_ref, o_ref):
    blk = MOBA_BLOCK
    seq = k_ref.shape[0]
    n_blk = seq // blk
    slope = slopes_ref[pl.program_id(1)]

    k_mean = jnp.sum(k_ref[...].astype(F32).reshape(n_blk, blk, HEAD_DIM), axis=1) * (1.0 / blk)
    k_mean = jnp.concatenate([k_mean, jnp.zeros((HEAD_DIM - n_blk, HEAD_DIM), F32)], axis=0)
    gate_all = lax.dot_general(q_ref[...].astype(F32), k_mean, (((1,), (1,)), ((), ())),
                               precision=lax.Precision.HIGHEST, preferred_element_type=F32)
    lane = lax.broadcasted_iota(jnp.int32, (blk, HEAD_DIM), 1)
    qi = lax.broadcasted_iota(jnp.int32, (blk, blk), 0)
    ki = lax.broadcasted_iota(jnp.int32, (blk, blk), 1)
    causal = qi >= ki
    bias0 = slope * (qi - ki).astype(F32)

    for i in range(n_blk):
        rows = slice(i * blk, (i + 1) * blk)
        q = q_ref[rows, :]
        if i > MOBA_TOPK:
            gate = jnp.where(lane < i, gate_all[rows, :], NEG_INF)
            rank = jnp.zeros((blk, HEAD_DIM), jnp.int32)
            for m in range(i):
                col = gate[:, m:m + 1]
                ahead = (col > gate) | ((col == gate) & (lane > m))
                rank = rank + ahead.astype(jnp.int32)
        scores = []
        for n in range(i + 1):
            kn = k_ref[n * blk:(n + 1) * blk, :]
            s = lax.dot_general(q, kn, (((1,), (1,)), ((), ())), preferred_element_type=F32) * ATTN_SCALE
            s = s - (bias0 + slope * float((i - n) * blk))
            if n == i:
                s = jnp.where(causal, s, NEG_INF)
            elif i > MOBA_TOPK:
                s = jnp.where(rank[:, n:n + 1] < MOBA_TOPK, s, NEG_INF)
            scores.append(s)
        m_row = jnp.max(functools.reduce(jnp.maximum, scores), axis=-1, keepdims=True)
        p_sum = None
        acc = jnp.zeros((blk, HEAD_DIM), F32)
        for n, s in enumerate(scores):
            p = jnp.exp(s - m_row)
            p_sum = p if p_sum is None else p_sum + p
            acc = acc + jnp.dot(p.astype(BF16), v_ref[n * blk:(n + 1) * blk, :],
                                preferred_element_type=F32)
        den = jnp.sum(p_sum, axis=-1, keepdims=True)
        o_ref[rows, :] = (acc / den).astype(o_ref.dtype)


def _moba_mixer(z_d, bsz, seq):
    width = BRANCH_WIDTH
    z3 = z_d.reshape(bsz, seq, 3 * width)
    slopes = jnp.asarray(_alibi_slopes()[len(DIL_DILATIONS)::ALIBI_SETS])
    spec = lambda off: pl.BlockSpec((None, seq, HEAD_DIM), lambda b, h: (b, 0, off + h))
    out = pl.pallas_call(
        _moba_kernel,
        grid=(bsz, N_HEADS),
        in_specs=[pl.BlockSpec(memory_space=pltpu.SMEM), spec(0), spec(N_HEADS), spec(2 * N_HEADS)],
        out_specs=spec(0),
        out_shape=jax.ShapeDtypeStruct((bsz, seq, width), BF16),
        compiler_params=_params(2, 32 * MIB),
        name="moba",
    )(slopes, z3, z3, z3)
    return out.reshape(bsz * seq, width)


def _merge_kernel(h_ref, ya_ref, yb_ref, yc_ref, yd_ref, mw0, mw1, mw2, mw3, mb_ref, bw_ref, o_ref):
    h = h_ref[...]
    acc = None
    for b, (y_ref, mw_ref) in enumerate(zip((ya_ref, yb_ref, yc_ref, yd_ref), (mw0, mw1, mw2, mw3))):
        gate = jnp.dot(h, mw_ref[...], preferred_element_type=F32) + mb_ref[b:b + 1, :]
        proj = jnp.dot(y_ref[...], bw_ref[b], preferred_element_type=F32)
        term = jax.nn.sigmoid(gate) * proj
        acc = term if acc is None else acc + term
    o_ref[...] = acc.astype(o_ref.dtype)


def _merge(h, ys, merge_ws, merge_b, branch_w, layer, casts=()):
    t, d = h.shape
    width = BRANCH_WIDTH
    tm, tn = 1024, 256
    n_j = d // tn
    mw_spec = pl.BlockSpec((d, tn), lambda i, j: (0, j))
    y_spec = _single_buffered((tm, width), lambda i, j: (i, 0))
    vmem = (tm * d * 2 + 4 * tm * width * 2 + 2 * 4 * d * tn * 2 + 2 * 4 * width * tn * 2
            + 6 * tm * tn * 4 + VMEM_SLACK)
    (out,), cast_outs = _host_call(
        _merge_kernel, (t // tm, n_j),
        [_single_buffered((tm, d), lambda i, j: (i, 0)),
         y_spec, y_spec, y_spec, y_spec,
         mw_spec, mw_spec, mw_spec, mw_spec,
         pl.BlockSpec((None, 4, tn), lambda i, j: (layer, 0, j)),
         pl.BlockSpec((4, width, tn), lambda i, j: (0, 0, j))],
        [pl.BlockSpec((tm, tn), lambda i, j: (i, j))],
        [jax.ShapeDtypeStruct((t, d), BF16)],
        (h, *ys, *merge_ws, merge_b, branch_w), casts, vmem, "merge")
    return out, cast_outs


def _ffn_up_kernel(h_ref, wg_ref, wu_ref, cw_ref, cb_ref, o_ref):
    seq = h_ref.shape[0]
    parts = [slice(r, r + FFN_ROW_SPLIT) for r in range(0, seq, FFN_ROW_SPLIT)]

    def project(w_ref):
        return jnp.concatenate([jnp.dot(h_ref[rows, :], w_ref[...], preferred_element_type=F32)
                                for rows in parts], axis=0)

    g = project(wg_ref)
    row = lax.broadcasted_iota(jnp.int32, g.shape, 0)
    g1 = jnp.where(row >= 1, pltpu.roll(g, 1, 0), 0.0)
    g2 = jnp.where(row >= 2, pltpu.roll(g, 2, 0), 0.0)
    a = cw_ref[2:3, :] * g + cw_ref[1:2, :] * g1 + cw_ref[0:1, :] * g2 + cb_ref[...]
    u = project(wu_ref)
    o_ref[...] = (jax.nn.gelu(a) * u).astype(o_ref.dtype)


def _ffn_up(h2, wg, wu, conv_w, conv_b, layer, bsz, seq, casts=()):
    t, d = h2.shape
    f = wg.shape[1]
    tn = 256
    assert seq % FFN_ROW_SPLIT == 0
    w_spec = pl.BlockSpec((d, tn), lambda b, j: (0, j))
    vmem = seq * d * 2 + 2 * 2 * d * tn * 2 + 8 * seq * tn * 4 + VMEM_SLACK
    (out,), cast_outs = _host_call(
        _ffn_up_kernel, (bsz, f // tn),
        [_single_buffered((seq, d), lambda b, j: (b, 0)),
         w_spec, w_spec,
         pl.BlockSpec((None, 3, tn), lambda b, j: (layer, 0, j)),
         pl.BlockSpec((None, 1, tn), lambda b, j: (layer, 0, j))],
        [pl.BlockSpec((seq, tn), lambda b, j: (b, j))],
        [jax.ShapeDtypeStruct((t, f), BF16)],
        (h2, wg, wu, conv_w, conv_b), casts, vmem, "ffn_up")
    return out, cast_outs


def kernel(x, c, norm1_g, w_ada, b_ada, w_in, sgu_norm_g, sgu_w, sgu_b, pool_w, pool_scale, merge_w,
           merge_b, branch_w, out_w, norm2_g, ffn_wg, ffn_wu, conv_w, conv_b, ffn_wd, final_g):
    bsz, seq, d = x.shape
    depth = w_in.shape[0]
    width = BRANCH_WIDTH
    t = bsz * seq
    assert seq % 512 == 0 and seq // DIL_DILATIONS[-1] >= DIL_BLOCK and d == 4 * width

    n_groups = len(DIL_DILATIONS)
    w_in_runs = ([(0, 1, 3 * width, 3 * width)]
                 + [((3 + g) * width, 3, width, n_groups * width) for g in range(n_groups)]
                 + [((3 + 3 * n_groups) * width, 1, 3 * width, 3 * width)])
    w_parts = [_cast_standalone(w_in, 0, w_in_runs[0])] + [None] * (len(w_in_runs) - 1)
    pool_w_b = pool_w.astype(BF16)
    branch_w2 = branch_w.reshape(depth, 4 * width, d)
    sgu_b_t = sgu_b.transpose(0, 2, 1)
    c_pad = jnp.zeros((8, d), BF16).at[:bsz].set(c.astype(BF16))
    row3 = lambda a: a.reshape(depth, 1, a.shape[-1])

    xf = x.reshape(t, d)
    mod_rows = _ada_mod(c_pad, w_ada, row3(b_ada), 0)
    for layer in range(depth):
        mod = mod_rows[:bsz].reshape(bsz * N_MOD, 1, d)

        h, *h_res = _norm_mod(xf, row3(norm1_g), layer, mod, 0, 1, seq, DIL_DILATIONS[1:])
        h_by_group = [h] + [hr.reshape(t, d) for hr in h_res]
        casts = [(branch_w2, layer)] + ([(w_in, layer, w_in_runs[1])] if w_parts[1] is None else [])
        z_ab, cast_outs = _matmul(h, w_parts[0], F32, 1024, 512, casts=casts)
        branch_w_b = cast_outs[0]
        if w_parts[1] is None:
            w_parts[1] = cast_outs[1]
        z_cd, merge_ws = [], []
        for p in range(1, len(w_parts)):
            a = h_by_group[p - 1] if p <= n_groups else h
            nxt = p + 1
            casts = [(w_in, layer, w_in_runs[nxt])] if nxt < len(w_parts) and w_parts[nxt] is None else []
            z_p, cast_outs = _matmul(a, w_parts[p], BF16, 1024, 1024, casts=casts,
                                     branch=(merge_w, layer, p - 1))
            if casts:
                w_parts[nxt] = cast_outs[0]
            z_cd.append(z_p)
            merge_ws.append(cast_outs[-1])
        z_groups, z_d = z_cd[:n_groups], z_cd[n_groups]
        y_a = _sgu(z_ab, row3(sgu_norm_g), sgu_w, sgu_b_t, layer)
        y_b = _pool(z_ab, pool_w_b, row3(pool_scale), layer, bsz, seq)
        y_c = _dilated_mixer(z_groups, bsz, seq)
        y_d = _moba_mixer(z_d, bsz, seq)
        merged, (ffn_wg_b, out_w_b) = _merge(h, (y_a, y_b, y_c, y_d), merge_ws, merge_b,
                                             branch_w_b.reshape(4, width, d), layer,
                                             casts=[(ffn_wg, layer), (out_w, layer)])
        xf, _, (ffn_wu_b,) = _matmul_residual(merged, out_w_b, xf, mod, 2, seq, 1024, 512,
                                              casts=[(ffn_wu, layer)], a_buffers=2)

        h2 = _norm_mod(xf, row3(norm2_g), layer, mod, 3, 4, seq)
        has_next = layer + 1 < depth
        next_w_in = [(w_in, layer + 1, runs) for runs in w_in_runs] if has_next else []
        f, cast_outs = _ffn_up(h2, ffn_wg_b, ffn_wu_b, conv_w, row3(conv_b), layer, bsz, seq,
                               casts=[(ffn_wd, layer)] + next_w_in)
        ffn_wd_b = cast_outs[0]
        w_parts = list(cast_outs[1:])
        xf, mod_rows, _ = _matmul_residual(f, ffn_wd_b, xf, mod, 5, seq, 512, 512, a_buffers=2,
                                           ada=(c_pad, w_ada, row3(b_ada), layer + 1) if has_next else None)

    return _final_norm(xf, final_g.reshape(1, d)).reshape(bsz, seq, d)
```
